```python
import jax, jax.numpy as jnp
from jax import lax
import numpy as np

D_MODEL = 2048
BATCH = 1
SEQ = 8192
DEPTH = 4

N_A_LAYERS = DEPTH // 2
N_B_LAYERS = DEPTH - N_A_LAYERS
EPS = 1e-6
MOD_INIT = 0.1
N_MOD = 6

CHUNK = 128
SGU_WIDTH = 2 * D_MODEL
SGU_GROUPS = 16
SGU_GROUP_DIM = SGU_WIDTH // SGU_GROUPS

D_FF = 11 * D_MODEL // 4
CONV_W = 3

N_HEADS = 16
HEAD_DIM = D_MODEL // N_HEADS
N_KV = 4
GROUP = N_HEADS // N_KV
N_BRANCH = 3
N_KV_TENSORS = 2 * N_BRANCH
CMP_LEN = 32
CMP_STRIDE = 16
SLC_LEN = 64
SLC_TOPK = 16
N_FORCED_LOCAL = 2
WINDOW = 512
Q_BLOCK = 128
NEG_INF = -1e30
FORCE_BONUS = 1e4

kernel_name = 'yoco_sgu_nsa_hybrid'


def alibi_slopes():
    h = jnp.arange(1, N_HEADS + 1, dtype=jnp.float32)
    return (2.0 ** (-8.0 * h / N_HEADS)).reshape(N_KV, GROUP)


def rms_norm(x, gain):
    xf = x.astype(jnp.float32)
    y = xf * lax.rsqrt(jnp.mean(xf * xf, axis=-1, keepdims=True) + EPS)
    return (y * gain.astype(jnp.float32)).astype(x.dtype)


def adaln(c, w, b, n):
    m = jax.nn.silu(c) @ w + b
    return jnp.split(m[:, None, :], n, axis=-1)


def modulate(x, gain, shift, scale):
    return rms_norm(x, gain) * (1 + scale) + shift


def masked_softmax(s, valid):
    s = jnp.where(valid, s, NEG_INF)
    m = jnp.max(s, axis=-1, keepdims=True)
    e = jnp.where(valid, jnp.exp(s - m), 0.0)
    return e / jnp.maximum(jnp.sum(e, axis=-1, keepdims=True), 1e-30)


def conv_ffn(h, w_up, conv_w, conv_b, w_down):
    T = h.shape[1]
    z = h @ w_up
    zp = jnp.pad(z, ((0, 0), (CONV_W - 1, 0), (0, 0)))
    z = conv_b + sum(conv_w[k] * zp[:, k:k + T] for k in range(CONV_W))
    a, v = jnp.split(z, 2, axis=-1)
    return (jax.nn.silu(a) * v) @ w_down


def chunked_sgu(h, w_in, v_gain, w_s, b_s, w_out):
    B, T, _ = h.shape
    u, v = jnp.split(jax.nn.gelu(h @ w_in), 2, axis=-1)
    v = rms_norm(v, v_gain).reshape(B, T // CHUNK, CHUNK, SGU_GROUPS, SGU_GROUP_DIM)
    w_causal = jnp.where(jnp.tril(jnp.ones((CHUNK, CHUNK), dtype=bool)), w_s, 0.0)
    sv = jnp.einsum('gts,bnsgc->bntgc', w_causal, v) + b_s.T[:, :, None]
    return (u * sv.reshape(B, T, SGU_WIDTH)) @ w_out


def nsa_shared_kv(x, c, gain, mod_w, mod_b, w_kv, cmp_pos, cmp_w1, cmp_w2, k_gain):
    B, T, _ = x.shape
    shift, scale = adaln(c, mod_w, mod_b, 2)
    h = modulate(x, gain, shift, scale)
    kv = (h @ w_kv).reshape(B, T, N_KV_TENSORS, N_KV, HEAD_DIM)
    n_cmp = T // CMP_STRIDE - 1

    def compress(t, i):
        s = t.reshape(B, T // CMP_STRIDE, CMP_STRIDE, N_KV, HEAD_DIM)
        blocks = jnp.concatenate([s[:, :-1], s[:, 1:]], axis=2) + cmp_pos[i][:, None, :]
        flat = blocks.transpose(0, 1, 3, 2, 4).reshape(B, n_cmp, N_KV, CMP_LEN * HEAD_DIM)
        return jax.nn.gelu(flat @ cmp_w1[i]) @ cmp_w2[i]

    k_cmp = rms_norm(compress(kv[:, :, 0], 0), k_gain[0])
    v_cmp = compress(kv[:, :, 1], 1)
    k_slc = rms_norm(kv[:, :, 2], k_gain[1])
    v_slc = kv[:, :, 3]
    k_win = rms_norm(kv[:, :, 4], k_gain[2])
    v_win = kv[:, :, 5]
    return (k_cmp, v_cmp, k_slc, v_slc, k_win, v_win)


def nsa_attention(h, kv, w_in, q_gain, w_o):
    k_cmp, v_cmp, k_slc, v_slc, k_win, v_win = kv
    B, T, _ = h.shape
    f32 = jnp.float32
    proj = h @ w_in
    q = rms_norm(proj[..., :N_HEADS * HEAD_DIM].reshape(B, T, N_KV, GROUP, HEAD_DIM), q_gain)
    gates = jax.nn.sigmoid(proj[..., N_HEADS * HEAD_DIM:].astype(f32)).reshape(B, T, N_BRANCH, N_KV, GROUP)
    sl = alibi_slopes()[None, :, :, None, None]
    scale = HEAD_DIM ** -0.5
    n_cmp = k_cmp.shape[1]
    n_slc = T // SLC_LEN
    top_k = min(SLC_TOPK, n_slc)
    cmp_start = jnp.arange(n_cmp) * CMP_STRIDE
    cmp_end = cmp_start + CMP_LEN - 1
    blk = jnp.arange(n_slc)
    slc_start = blk * SLC_LEN
    overlap = ((cmp_start[:, None] < slc_start[None, :] + SLC_LEN)
               & (cmp_start[:, None] + CMP_LEN > slc_start[None, :])).astype(f32)
    k_blocks = k_slc.reshape(B, n_slc, SLC_LEN, N_KV, HEAD_DIM).transpose(0, 3, 1, 2, 4)
    v_blocks = v_slc.reshape(B, n_slc, SLC_LEN, N_KV, HEAD_DIM).transpose(0, 3, 1, 2, 4)
    k_win_p = jnp.pad(k_win, ((0, 0), (WINDOW, 0), (0, 0), (0, 0)))
    v_win_p = jnp.pad(v_win, ((0, 0), (WINDOW, 0), (0, 0), (0, 0)))
    b_ix = jnp.arange(B)[:, None, None, None]
    g_ix = jnp.arange(N_KV)[None, :, None, None]
    tok = jnp.arange(SLC_LEN)
    n_sel = top_k * SLC_LEN

    def query_block(qb):
        t0 = qb * Q_BLOCK
        qi = lax.dynamic_slice_in_dim(q, t0, Q_BLOCK, axis=1)
        gi = lax.dynamic_slice_in_dim(gates, t0, Q_BLOCK, axis=1)
        t = t0 + jnp.arange(Q_BLOCK)
        d_cmp = t[:, None] - cmp_end[None, :]
        s = jnp.einsum('bqgrd,bngd->bgrqn', qi, k_cmp, preferred_element_type=f32) * scale - sl * d_cmp
        p_cmp = masked_softmax(s, d_cmp >= 0)
        o_cmp = jnp.einsum('bgrqn,bngd->bqgrd', p_cmp, v_cmp.astype(f32))
        imp = jnp.einsum('bgrqn,ns->bgqs', p_cmp, overlap)
        cur = (t // SLC_LEN)[:, None]
        forced = (blk[None, :] == 0) | ((blk[None, :] <= cur) & (blk[None, :] > cur - N_FORCED_LOCAL))
        imp = jnp.where(forced, imp + FORCE_BONUS, imp)
        imp = jnp.where(blk[None, :] <= cur, imp, NEG_INF)
        _, idx = lax.top_k(imp, top_k)
        ks = k_blocks[b_ix, g_ix, idx].reshape(B, N_KV, Q_BLOCK, n_sel, HEAD_DIM)
        vs = v_blocks[b_ix, g_ix, idx].reshape(B, N_KV, Q_BLOCK, n_sel, HEAD_DIM)
        d_slc = (t[None, None, :, None, None] - (idx[..., None] * SLC_LEN + tok)).reshape(B, N_KV, 1, Q_BLOCK, n_sel)
        s = jnp.einsum('bqgrd,bgqkd->bgrqk', qi, ks, preferred_element_type=f32) * scale - sl * d_slc
        p = masked_softmax(s, d_slc >= 0)
        o_slc = jnp.einsum('bgrqk,bgqkd->bqgrd', p, vs.astype(f32))
        kw = lax.dynamic_slice_in_dim(k_win_p, t0, Q_BLOCK + WINDOW, axis=1)
        vw = lax.dynamic_slice_in_dim(v_win_p, t0, Q_BLOCK + WINDOW, axis=1)
        sp = t0 - WINDOW + jnp.arange(Q_BLOCK + WINDOW)
        d_win = t[:, None] - sp[None, :]
        valid = (d_win >= 0) & (d_win < WINDOW) & (sp[None, :] >= 0)
        s = jnp.einsum('bqgrd,bkgd->bgrqk', qi, kw, preferred_element_type=f32) * scale - sl * d_win
        p = masked_softmax(s, valid)
        o_win = jnp.einsum('bgrqk,bkgd->bqgrd', p, vw.astype(f32))
        g = gi[..., None]
        return g[:, :, 0] * o_cmp + g[:, :, 1] * o_slc + g[:, :, 2] * o_win

    out = lax.map(query_block, jnp.arange(T // Q_BLOCK))
    out = out.transpose(1, 0, 2, 3, 4, 5).reshape(B, T, N_HEADS * HEAD_DIM).astype(h.dtype)
    return out @ w_o


def setup_inputs(seed: int = 0) -> dict:
    key = jax.random.key(seed)
    ks = jax.random.split(key, 32)
    f32 = jnp.float32

    def dense(k, shape, fan_in, mult=1.0):
        return jax.random.normal(k, shape, f32) * (mult * fan_in ** -0.5)

    def near_one(k, shape):
        return 1.0 + 0.02 * jax.random.normal(k, shape, f32)

    def small(k, shape):
        return 0.02 * jax.random.normal(k, shape, f32)

    gate_offset = jnp.zeros((N_MOD, D_MODEL), f32).at[jnp.array([2, 5])].set(1.0).reshape(N_MOD * D_MODEL)
    return {
        'x': jax.random.normal(ks[0], (BATCH, SEQ, D_MODEL), f32),
        'c': jax.random.normal(ks[1], (BATCH, D_MODEL), f32),
        'mod_w': dense(ks[2], (DEPTH, D_MODEL, N_MOD * D_MODEL), D_MODEL, MOD_INIT),
        'mod_b': small(ks[3], (DEPTH, N_MOD * D_MODEL)) + gate_offset,
        'norm_gain': near_one(ks[4], (DEPTH, 2, D_MODEL)),
        'ffn_w_up': dense(ks[5], (DEPTH, D_MODEL, 2 * D_FF), D_MODEL),
        'ffn_conv_w': dense(ks[6], (DEPTH, CONV_W, 2 * D_FF), CONV_W),
        'ffn_conv_b': small(ks[7], (DEPTH, 2 * D_FF)),
        'ffn_w_down': dense(ks[8], (DEPTH, D_FF, D_MODEL), D_FF),
        'sgu_w_in': dense(ks[9], (N_A_LAYERS, D_MODEL, 2 * SGU_WIDTH), D_MODEL),
        'sgu_v_gain': near_one(ks[10], (N_A_LAYERS, SGU_WIDTH)),
        'sgu_w_s': dense(ks[11], (N_A_LAYERS, SGU_GROUPS, CHUNK, CHUNK), CHUNK),
        'sgu_b_s': near_one(ks[12], (N_A_LAYERS, SGU_GROUPS, CHUNK)),
        'sgu_w_out': dense(ks[13], (N_A_LAYERS, SGU_WIDTH, D_MODEL), SGU_WIDTH),
        'kv_gain': near_one(ks[14], (D_MODEL,)),
        'kv_mod_w': dense(ks[15], (D_MODEL, 2 * D_MODEL), D_MODEL, MOD_INIT),
        'kv_mod_b': small(ks[16], (2 * D_MODEL,)),
        'w_kv': dense(ks[17], (D_MODEL, N_KV_TENSORS * N_KV * HEAD_DIM), D_MODEL),
        'cmp_pos': 0.5 * jax.random.normal(ks[18], (2, CMP_LEN, HEAD_DIM), f32),
        'cmp_w1': dense(ks[19], (2, CMP_LEN * HEAD_DIM, HEAD_DIM), CMP_LEN * HEAD_DIM),
        'cmp_w2': dense(ks[20], (2, HEAD_DIM, HEAD_DIM), HEAD_DIM),
        'k_gain': near_one(ks[21], (N_BRANCH, HEAD_DIM)),
        'nsa_w_in': dense(ks[22], (N_B_LAYERS, D_MODEL, N_HEADS * HEAD_DIM + N_BRANCH * N_HEADS), D_MODEL),
        'nsa_q_gain': near_one(ks[23], (N_B_LAYERS, HEAD_DIM)),
        'nsa_w_o': dense(ks[24], (N_B_LAYERS, N_HEADS * HEAD_DIM, D_MODEL), N_HEADS * HEAD_DIM),
    }


def reference(x, c, mod_w, mod_b, norm_gain, ffn_w_up, ffn_conv_w, ffn_conv_b, ffn_w_down,
              sgu_w_in, sgu_v_gain, sgu_w_s, sgu_b_s, sgu_w_out,
              kv_gain, kv_mod_w, kv_mod_b, w_kv, cmp_pos, cmp_w1, cmp_w2, k_gain,
              nsa_w_in, nsa_q_gain, nsa_w_o):
    shared_kv = None
    for layer in range(DEPTH):
        if layer == N_A_LAYERS:
            shared_kv = nsa_shared_kv(x, c, kv_gain, kv_mod_w, kv_mod_b, w_kv, cmp_pos, cmp_w1, cmp_w2, k_gain)
        sh1, sc1, g1, sh2, sc2, g2 = adaln(c, mod_w[layer], mod_b[layer], N_MOD)
        h = modulate(x, norm_gain[layer, 0], sh1, sc1)
        if layer < N_A_LAYERS:
            a = layer
            y = chunked_sgu(h, sgu_w_in[a], sgu_v_gain[a], sgu_w_s[a], sgu_b_s[a], sgu_w_out[a])
        else:
            bl = layer - N_A_LAYERS
            y = nsa_attention(h, shared_kv, nsa_w_in[bl], nsa_q_gain[bl], nsa_w_o[bl])
        x = x + g1 * y
        h = modulate(x, norm_gain[layer, 1], sh2, sc2)
        x = x + g2 * conv_ffn(h, ffn_w_up[layer], ffn_conv_w[layer], ffn_conv_b[layer], ffn_w_down[layer])
    return x
```

```python
import functools

import jax
import jax.numpy as jnp
from jax import lax
from jax.experimental import pallas as pl
from jax.experimental.pallas import tpu as pltpu

F32 = jnp.float32
BF16 = jnp.bfloat16

D_MODEL = 2048
DEPTH = 4
N_A_LAYERS = 2
EPS = 1e-6
N_MOD = 6

CHUNK = 128
SGU_WIDTH = 2 * D_MODEL
SGU_GROUPS = 16
SGU_GROUP_DIM = SGU_WIDTH // SGU_GROUPS

D_FF = 11 * D_MODEL // 4
CONV_W = 3
CONV_HALO = 16

N_HEADS = 16
HEAD_DIM = 128
N_KV = 4
GROUP = 4
N_BRANCH = 3
N_KV_TENSORS = 6
CMP_LEN = 32
CMP_STRIDE = 16
SLC_LEN = 64
SLC_TOPK = 16
N_FORCED_LOCAL = 2
WINDOW = 512
Q_BLOCK = 128
NEG_INF = -1e30
FORCE_BONUS = 1e4
QK_SCALE = HEAD_DIM ** -0.5

KV_TILE = 512
WIN_KEYS = WINDOW + Q_BLOCK

MIB = 1024 * 1024


def _params(sem, vmem_mib):
    return pltpu.CompilerParams(dimension_semantics=sem, vmem_limit_bytes=vmem_mib * MIB)


def _sigmoid(x):
    return 1.0 / (1.0 + jnp.exp(-x))


def _gelu_tanh(x):
    return 0.5 * x * (1.0 + jnp.tanh(0.7978845608028654 * (x + 0.044715 * (x * x * x))))


def _modnorm(x, gain, scale, shift):
    r = lax.rsqrt(jnp.mean(x * x, axis=-1, keepdims=True) + EPS)
    return (x * r) * gain * (1.0 + scale) + shift


def _head_rms(z, gain):
    return z * lax.rsqrt(jnp.mean(z * z, axis=-1, keepdims=True) + EPS) * gain


def _dot(a, b):
    return jnp.dot(a, b, preferred_element_type=F32)


def _dot_nt(a, b):
    return lax.dot_general(a, b, (((1,), (1,)), ((), ())), preferred_element_type=F32)


def _adaln_kernel(c_ref, w_ref, b_ref, o_ref):
    c = c_ref[...]
    s = c * _sigmoid(c)
    o_ref[0] = jnp.sum(s * w_ref[0], axis=0, keepdims=True) + b_ref[0]


def _adaln(c_col, w, b, tn=1024):
    L, D, N = w.shape
    return pl.pallas_call(
        _adaln_kernel,
        grid=(L, N // tn),
        in_specs=[
            pl.BlockSpec((D, 1), lambda l, j: (0, 0)),
            pl.BlockSpec((1, D, tn), lambda l, j: (l, 0, j)),
            pl.BlockSpec((1, 1, tn), lambda l, j: (l, 0, j)),
        ],
        out_specs=pl.BlockSpec((1, 1, tn), lambda l, j: (l, 0, j)),
        out_shape=jax.ShapeDtypeStruct((L, 1, N), F32),
        compiler_params=_params(("arbitrary", "arbitrary"), 40),
        name="adaln",
    )(c_col, w, b.reshape(L, 1, N))


def _sgu_in_kernel(x_ref, g_ref, sc_ref, sh_ref, w_ref, uv_ref, ss_ref, h_scr, *, n_u_tiles):
    j = pl.program_id(1)

    @pl.when(j == 0)
    def _():
        h_scr[...] = _modnorm(x_ref[...], g_ref[...], sc_ref[...], sh_ref[...]).astype(BF16)
        ss_ref[...] = jnp.zeros_like(ss_ref)

    y = _gelu_tanh(_dot(h_scr[...], w_ref[...]))
    uv_ref[...] = y.astype(BF16)

    @pl.when(j >= n_u_tiles)
    def _():
        ss_ref[...] += jnp.sum(y * y, axis=-1, keepdims=True)


def _sgu_in(x, gain, scale, shift, w, tm=1024, tn=512):
    T, D = x.shape
    N = w.shape[1]
    vec = pl.BlockSpec((1, D), lambda i, j: (0, 0))
    return pl.pallas_call(
        functools.partial(_sgu_in_kernel, n_u_tiles=SGU_WIDTH // tn),
        grid=(T // tm, N // tn),
        in_specs=[pl.BlockSpec((tm, D), lambda i, j: (i, 0)), vec, vec, vec,
                  pl.BlockSpec((D, tn), lambda i, j: (0, j))],
        out_specs=[pl.BlockSpec((tm, tn), lambda i, j: (i, j)),
                   pl.BlockSpec((tm, 1), lambda i, j: (i, 0))],
        out_shape=[jax.ShapeDtypeStruct((T, N), BF16), jax.ShapeDtypeStruct((T, 1), F32)],
        scratch_shapes=[pltpu.VMEM((tm, D), BF16)],
        compiler_params=_params(("arbitrary", "arbitrary"), 48),
        name="sgu_in",
    )(x, gain, scale, shift, w)


def _sgu_gate_kernel(u_ref, v_ref, ss_ref, vg_ref, w_ref, b_ref, o_ref, *, n_chunks):
    row = lax.broadcasted_iota(jnp.int32, (CHUNK, CHUNK), 0)
    col = lax.broadcasted_iota(jnp.int32, (CHUNK, CHUNK), 1)
    wc = jnp.where(row >= col, w_ref[0], 0.0).astype(BF16)
    r = lax.rsqrt(ss_ref[...] * (1.0 / SGU_WIDTH) + EPS)
    v = (v_ref[...].astype(F32) * r * vg_ref[...]).astype(BF16)
    b = b_ref[0]
    for n in range(n_chunks):
        rows = slice(n * CHUNK, (n + 1) * CHUNK)
        sv = _dot(wc, v[rows]) + b
        o_ref[rows, :] = (u_ref[rows, :].astype(F32) * sv).astype(BF16)


def _sgu_gate(uv, ss, v_gain, w_s, b_s, tc=512):
    T = uv.shape[0]
    gd = SGU_GROUP_DIM
    return pl.pallas_call(
        functools.partial(_sgu_gate_kernel, n_chunks=tc // CHUNK),
        grid=(SGU_GROUPS, T // tc),
        in_specs=[
            pl.BlockSpec((tc, gd), lambda g, i: (i, g)),
            pl.BlockSpec((tc, gd), lambda g, i: (i, SGU_GROUPS + g)),
            pl.BlockSpec((tc, 1), lambda g, i: (i, 0)),
            pl.BlockSpec((1, gd), lambda g, i: (0, g)),
            pl.BlockSpec((1, CHUNK, CHUNK), lambda g, i: (g, 0, 0)),
            pl.BlockSpec((1, CHUNK, 1), lambda g, i: (g, 0, 0)),
        ],
        out_specs=pl.BlockSpec((tc, gd), lambda g, i: (i, g)),
        out_shape=jax.ShapeDtypeStruct((T, SGU_WIDTH), BF16),
        compiler_params=_params(("arbitrary", "arbitrary"), 32),
        name="sgu_gate",
    )(uv, uv, ss, v_gain.reshape(1, SGU_WIDTH), w_s, b_s.reshape(SGU_GROUPS, CHUNK, 1))


def _mm_res_kernel(a_ref, w_ref, x_ref, g_ref, o_ref):
    o_ref[...] = x_ref[...] + g_ref[...] * _dot(a_ref[...], w_ref[...])


def _mm_res(a, w, x, gate, tm=512, tn=512):
    T, K = a.shape
    N = w.shape[1]
    return pl.pallas_call(
        _mm_res_kernel,
        grid=(T // tm, N // tn),
        in_specs=[pl.BlockSpec((tm, K), lambda i, j: (i, 0)),
                  pl.BlockSpec((K, tn), lambda i, j: (0, j)),
                  pl.BlockSpec((tm, tn), lambda i, j: (i, j)),
                  pl.BlockSpec((1, tn), lambda i, j: (0, j))],
        out_specs=pl.BlockSpec((tm, tn), lambda i, j: (i, j)),
        out_shape=jax.ShapeDtypeStruct((T, N), F32),
        compiler_params=_params(("arbitrary", "arbitrary"), 48),
        name="mm_res",
    )(a, w, x, gate)


def _ffn_up_kernel(xh_ref, x_ref, g_ref, sc_ref, sh_ref, wa_ref, wv_ref, cwa_ref, cwv_ref,
                   cba_ref, cbv_ref, o_ref, h_scr, za_scr, zv_scr, *, tm):
    i = pl.program_id(0)
    j = pl.program_id(1)

    @pl.when(j == 0)
    def _():
        g, sc, sh = g_ref[...], sc_ref[...], sh_ref[...]
        h_scr[CONV_HALO:, :] = _modnorm(x_ref[...], g, sc, sh).astype(BF16)
        halo = _modnorm(xh_ref[...], g, sc, sh)
        h_scr[:CONV_HALO, :] = jnp.where(i > 0, halo, 0.0).astype(BF16)

    h = h_scr[...]
    za_scr[...] = _dot(h, wa_ref[...])
    zv_scr[...] = _dot(h, wv_ref[...])

    def conv(z_scr, cw_ref, cb_ref):
        acc = cb_ref[...] + cw_ref[0:1, :] * z_scr[pl.ds(CONV_HALO - 2, tm), :]
        acc = acc + cw_ref[1:2, :] * z_scr[pl.ds(CONV_HALO - 1, tm), :]
        return acc + cw_ref[2:3, :] * z_scr[pl.ds(CONV_HALO, tm), :]

    a = conv(za_scr, cwa_ref, cba_ref)
    v = conv(zv_scr, cwv_ref, cbv_ref)
    o_ref[...] = (a * _sigmoid(a) * v).astype(BF16)


def _ffn_up(x, gain, scale, shift, w_up, conv_w, conv_b, tm=1024, tn=512):
    T, D = x.shape
    nj = D_FF // tn
    hb = tm // CONV_HALO
    vec = pl.BlockSpec((1, D), lambda i, j: (0, 0))
    conv_b = conv_b.reshape(1, 2 * D_FF)
    return pl.pallas_call(
        functools.partial(_ffn_up_kernel, tm=tm),
        grid=(T // tm, nj),
        in_specs=[
            pl.BlockSpec((CONV_HALO, D), lambda i, j: (jnp.maximum(i * hb - 1, 0), 0)),
            pl.BlockSpec((tm, D), lambda i, j: (i, 0)),
            vec, vec, vec,
            pl.BlockSpec((D, tn), lambda i, j: (0, j)),
            pl.BlockSpec((D, tn), lambda i, j: (0, j + nj)),
            pl.BlockSpec((CONV_W, tn), lambda i, j: (0, j)),
            pl.BlockSpec((CONV_W, tn), lambda i, j: (0, j + nj)),
            pl.BlockSpec((1, tn), lambda i, j: (0, j)),
            pl.BlockSpec((1, tn), lambda i, j: (0, j + nj)),
        ],
        out_specs=pl.BlockSpec((tm, tn), lambda i, j: (i, j)),
        out_shape=jax.ShapeDtypeStruct((T, D_FF), BF16),
        scratch_shapes=[pltpu.VMEM((tm + CONV_HALO, D), BF16),
                        pltpu.VMEM((tm + CONV_HALO, tn), F32),
                        pltpu.VMEM((tm + CONV_HALO, tn), F32)],
        compiler_params=_params(("arbitrary", "arbitrary"), 52),
        name="ffn_up",
    )(x, x, gain, scale, shift, w_up, w_up, conv_w, conv_w, conv_b, conv_b)


def _kv_kernel(x_ref, g_ref, sc_ref, sh_ref, w_ref, kg_ref, o_ref, h_scr):
    j = pl.program_id(1)

    @pl.when(j == 0)
    def _():
        h_scr[...] = _modnorm(x_ref[...], g_ref[...], sc_ref[...], sh_ref[...]).astype(BF16)

    z = _dot(h_scr[...], w_ref[...])
    is_key_normed = (j == 2) | (j == 4)
    gain = kg_ref[0]
    for g in range(N_KV):
        zg = z[:, g * HEAD_DIM:(g + 1) * HEAD_DIM]
        o_ref[g] = jnp.where(is_key_normed, _head_rms(zg, gain), zg).astype(BF16)


def _kv_proj(x, gain, scale, shift, w, key_gain, tm=1024):
    T, D = x.shape
    tn = N_KV * HEAD_DIM
    vec = pl.BlockSpec((1, D), lambda i, j: (0, 0))
    return pl.pallas_call(
        _kv_kernel,
        grid=(T // tm, N_KV_TENSORS),
        in_specs=[pl.BlockSpec((tm, D), lambda i, j: (i, 0)), vec, vec, vec,
                  pl.BlockSpec((D, tn), lambda i, j: (0, j)),
                  pl.BlockSpec((1, 1, HEAD_DIM), lambda i, j: (j, 0, 0))],
        out_specs=pl.BlockSpec((N_KV, tm, HEAD_DIM), lambda i, j: (j, i, 0)),
        out_shape=jax.ShapeDtypeStruct((N_KV_TENSORS * N_KV, T, HEAD_DIM), BF16),
        scratch_shapes=[pltpu.VMEM((tm, D), BF16)],
        compiler_params=_params(("arbitrary", "arbitrary"), 48),
        name="kv_proj",
    )(x, gain, scale, shift, w, key_gain)


def _cmp_kernel(seg_ref, pos_ref, w1_ref, w2_ref, kg_ref, o_ref, b_scr, *, n_seg):
    i = pl.program_id(0)
    seg = seg_ref[0]
    w1 = w1_ref[0]
    half = CMP_STRIDE * HEAD_DIM
    first = _dot(seg, w1[:half].astype(BF16))
    second = _dot(seg, w1[half:].astype(BF16))
    pos_bias = jnp.sum(pos_ref[0] * w1, axis=0, keepdims=True)
    b_scr[pl.ds(0, n_seg), :] = second
    b_scr[pl.ds(n_seg, 8), :] = jnp.zeros((8, HEAD_DIM), F32)
    pre = first + b_scr[pl.ds(1, n_seg), :] + pos_bias
    out = _dot(_gelu_tanh(pre).astype(BF16), w2_ref[0].astype(BF16))
    o_ref[0, 0] = jnp.where(i == 0, _head_rms(out, kg_ref[...]), out)


def _compress(kv, cmp_pos, cmp_w1, cmp_w2, k_gain_cmp):
    T = kv.shape[1]
    n_seg = T // CMP_STRIDE
    seg_w = CMP_STRIDE * HEAD_DIM
    segs = kv.reshape(kv.shape[0], n_seg, seg_w)
    flat = CMP_LEN * HEAD_DIM
    return pl.pallas_call(
        functools.partial(_cmp_kernel, n_seg=n_seg),
        grid=(2, N_KV),
        in_specs=[pl.BlockSpec((1, n_seg, seg_w), lambda i, g: (i * N_KV + g, 0, 0)),
                  pl.BlockSpec((1, flat, 1), lambda i, g: (i, 0, 0)),
                  pl.BlockSpec((1, flat, HEAD_DIM), lambda i, g: (i, 0, 0)),
                  pl.BlockSpec((1, HEAD_DIM, HEAD_DIM), lambda i, g: (i, 0, 0)),
                  pl.BlockSpec((1, HEAD_DIM), lambda i, g: (0, 0))],
        out_specs=pl.BlockSpec((1, 1, n_seg, HEAD_DIM), lambda i, g: (i, g, 0, 0)),
        out_shape=jax.ShapeDtypeStruct((2, N_KV, n_seg, HEAD_DIM), F32),
        scratch_shapes=[pltpu.VMEM((n_seg + 8, HEAD_DIM), F32)],
        compiler_params=_params(("arbitrary", "arbitrary"), 40),
        name="compress",
    )(segs, cmp_pos.reshape(2, flat, 1), cmp_w1, cmp_w2, k_gain_cmp.reshape(1, HEAD_DIM))


def _q_kernel(x_ref, g_ref, sc_ref, sh_ref, wq_ref, wg_ref, qg_ref, q_ref, gt_ref, h_scr):
    j = pl.program_id(1)

    @pl.when(j == 0)
    def _():
        h_scr[...] = _modnorm(x_ref[...], g_ref[...], sc_ref[...], sh_ref[...]).astype(BF16)
        gt_ref[...] = _sigmoid(_dot(h_scr[...], wg_ref[...]))

    z = _dot(h_scr[...], wq_ref[...])
    gain = qg_ref[...] * QK_SCALE
    for r in range(GROUP):
        cols = slice(r * HEAD_DIM, (r + 1) * HEAD_DIM)
        q_ref[:, cols] = _head_rms(z[:, cols], gain).astype(BF16)


def _q_proj(x, gain, scale, shift, w_q, w_gate, q_gain, tm=1024):
    T, D = x.shape
    tn = GROUP * HEAD_DIM
    vec = pl.BlockSpec((1, D), lambda i, j: (0, 0))
    ng = w_gate.shape[1]
    return pl.pallas_call(
        _q_kernel,
        grid=(T // tm, N_KV),
        in_specs=[pl.BlockSpec((tm, D), lambda i, j: (i, 0)), vec, vec, vec,
                  pl.BlockSpec((D, tn), lambda i, j: (0, j)),
                  pl.BlockSpec((D, ng), lambda i, j: (0, 0)),
                  pl.BlockSpec((1, HEAD_DIM), lambda i, j: (0, 0))],
        out_specs=[pl.BlockSpec((tm, tn), lambda i, j: (i, j)),
                   pl.BlockSpec((tm, ng), lambda i, j: (i, 0))],
        out_shape=[jax.ShapeDtypeStruct((T, N_HEADS * HEAD_DIM), BF16),
                   jax.ShapeDtypeStruct((T, ng), F32)],
        scratch_shapes=[pltpu.VMEM((tm, D), BF16)],
        compiler_params=_params(("arbitrary", "arbitrary"), 48),
        name="q_proj",
    )(x, gain, scale, shift, w_q, w_gate, q_gain.reshape(1, HEAD_DIM))


def _masked_softmax(s, valid):
    s = jnp.where(valid, s, NEG_INF)
    m = jnp.max(s, axis=-1, keepdims=True)
    e = jnp.where(valid, jnp.exp(s - m), 0.0)
    return e / jnp.maximum(jnp.sum(e, axis=-1, keepdims=True), 1e-30)


def _split3_bf16(x):
    hi = x.astype(BF16)
    r1 = x - hi.astype(F32)
    mid = r1.astype(BF16)
    lo = (r1 - mid.astype(F32)).astype(BF16)
    return hi, mid, lo


def _attn_kernel(q_ref, gt_ref, sl_ref, kc_ref, vc_ref, ks_ref, vs_ref, kw_ref, vw_ref, e_ref,
                 o_ref, imp_scr, m_scr, l_scr, acc_scr, *, n_cmp, n_slc):
    qb = pl.program_id(1)
    t0 = qb * Q_BLOCK
    rows = GROUP * Q_BLOCK
    slope = sl_ref[0]
    q = jnp.concatenate([q_ref[:, r * HEAD_DIM:(r + 1) * HEAD_DIM] for r in range(GROUP)], axis=0)

    def token_of_row(shape):
        return t0 + lax.broadcasted_iota(jnp.int32, shape, 0) % Q_BLOCK

    s = _dot_nt(q, kc_ref[0].astype(BF16))
    ci = lax.broadcasted_iota(jnp.int32, (rows, n_cmp), 1)
    d = (token_of_row((rows, n_cmp)) - (ci * CMP_STRIDE + (CMP_LEN - 1))).astype(F32)
    p = _masked_softmax(s - slope * d, d >= 0)
    o_cmp = _dot(p.astype(BF16), vc_ref[0].astype(BF16))

    p_group = p[0:Q_BLOCK]
    for r in range(1, GROUP):
        p_group = p_group + p[r * Q_BLOCK:(r + 1) * Q_BLOCK]
    oj = lax.broadcasted_iota(jnp.int32, (n_slc, n_cmp), 0) * (SLC_LEN // CMP_STRIDE)
    oi = lax.broadcasted_iota(jnp.int32, (n_slc, n_cmp), 1)
    overlap = ((oi > oj - CMP_LEN // CMP_STRIDE) & (oi < oj + SLC_LEN // CMP_STRIDE)).astype(BF16)
    imp = sum(_dot_nt(overlap, part) for part in _split3_bf16(p_group))
    blk = lax.broadcasted_iota(jnp.int32, (n_slc, Q_BLOCK), 0)
    cur = (t0 + lax.broadcasted_iota(jnp.int32, (n_slc, Q_BLOCK), 1)) // SLC_LEN
    forced = (blk == 0) | ((blk <= cur) & (blk > cur - N_FORCED_LOCAL))
    imp = jnp.where(forced, imp + FORCE_BONUS, imp)
    imp = jnp.where(blk <= cur, imp, NEG_INF)

    imp_scr[...] = imp

    def count_row(jp, cnt):
        other = jnp.broadcast_to(imp_scr[pl.ds(jp, 1), :], (n_slc, Q_BLOCK))
        beats = (other > imp) | ((other == imp) & (blk > jp))
        return cnt + jnp.where(beats, 1, 0)

    rank = lax.fori_loop(0, n_slc, count_row, jnp.zeros((n_slc, Q_BLOCK), jnp.int32))
    sel = (rank < SLC_TOPK).astype(F32).T.astype(BF16)

    m_scr[...] = jnp.full(m_scr.shape, NEG_INF, F32)
    l_scr[...] = jnp.zeros(l_scr.shape, F32)
    acc_scr[...] = jnp.zeros(acc_scr.shape, F32)
    tq = token_of_row((rows, KV_TILE))
    kcol = lax.broadcasted_iota(jnp.int32, (rows, KV_TILE), 1)

    def key_tile(kt, carry):
        k0 = pl.multiple_of(kt * KV_TILE, KV_TILE)
        s = _dot_nt(q, ks_ref[0, pl.ds(k0, KV_TILE), :])
        in_sel = _dot(sel, e_ref[:, pl.ds(k0, KV_TILE)])
        in_sel = jnp.concatenate([in_sel] * GROUP, axis=0)
        d = (tq - (k0 + kcol)).astype(F32)
        valid = (d >= 0) & (in_sel > 0.5)
        s = jnp.where(valid, s - slope * d, NEG_INF)
        m_old = m_scr[...]
        m_new = jnp.maximum(m_old, jnp.max(s, axis=-1, keepdims=True))
        e = jnp.where(valid, jnp.exp(s - m_new), 0.0)
        alpha = jnp.exp(m_old - m_new)
        l_scr[...] = alpha * l_scr[...] + jnp.sum(e, axis=-1, keepdims=True)
        acc_scr[...] = alpha * acc_scr[...] + _dot(e.astype(BF16), vs_ref[0, pl.ds(k0, KV_TILE), :])
        m_scr[...] = m_new
        return carry

    lax.fori_loop(0, (t0 + Q_BLOCK + KV_TILE - 1) // KV_TILE, key_tile, 0)
    o_slc = acc_scr[...] / jnp.maximum(l_scr[...], 1e-30)

    w0 = pl.multiple_of(jnp.maximum(t0 - WINDOW, 0), Q_BLOCK)
    s = _dot_nt(q, kw_ref[0, pl.ds(w0, WIN_KEYS), :])
    wcol = lax.broadcasted_iota(jnp.int32, (rows, WIN_KEYS), 1)
    d = (token_of_row((rows, WIN_KEYS)) - (w0 + wcol)).astype(F32)
    p = _masked_softmax(s - slope * d, (d >= 0) & (d < WINDOW))
    o_win = _dot(p.astype(BF16), vw_ref[0, pl.ds(w0, WIN_KEYS), :])

    gates = gt_ref[...]
    for r in range(GROUP):
        hr = slice(r * Q_BLOCK, (r + 1) * Q_BLOCK)
        o = (gates[:, r:r + 1] * o_cmp[hr]
             + gates[:, GROUP + r:GROUP + r + 1] * o_slc[hr]
             + gates[:, 2 * GROUP + r:2 * GROUP + r + 1] * o_win[hr])
        o_ref[:, r * HEAD_DIM:(r + 1) * HEAD_DIM] = o.astype(BF16)


def _attention(q, gates, slopes, cmp, kv, expand):
    T = q.shape[0]
    n_cmp = T // CMP_STRIDE
    n_slc = T // SLC_LEN
    rows = GROUP * Q_BLOCK
    hw = GROUP * HEAD_DIM

    def kv_spec(tensor):
        return pl.BlockSpec((1, T, HEAD_DIM), lambda g, qb: (tensor * N_KV + g, 0, 0))

    def cmp_spec(which):
        return pl.BlockSpec((1, n_cmp, HEAD_DIM), lambda g, qb: (which * N_KV + g, 0, 0))

    cmp2 = cmp.reshape(2 * N_KV, n_cmp, HEAD_DIM)
    return pl.pallas_call(
        functools.partial(_attn_kernel, n_cmp=n_cmp, n_slc=n_slc),
        grid=(N_KV, T // Q_BLOCK),
        in_specs=[pl.BlockSpec((Q_BLOCK, hw), lambda g, qb: (qb, g)),
                  pl.BlockSpec((Q_BLOCK, 128), lambda g, qb: (qb, g)),
                  pl.BlockSpec((1, rows, 1), lambda g, qb: (g, 0, 0)),
                  cmp_spec(0), cmp_spec(1), kv_spec(2), kv_spec(3), kv_spec(4), kv_spec(5),
                  pl.BlockSpec((n_slc, T), lambda g, qb: (0, 0))],
        out_specs=pl.BlockSpec((Q_BLOCK, hw), lambda g, qb: (qb, g)),
        out_shape=jax.ShapeDtypeStruct((T, N_HEADS * HEAD_DIM), BF16),
        scratch_shapes=[pltpu.VMEM((n_slc, Q_BLOCK), F32),
                        pltpu.VMEM((rows, 1), F32),
                        pltpu.VMEM((rows, 1), F32),
                        pltpu.VMEM((rows, HEAD_DIM), F32)],
        compiler_params=_params(("arbitrary", "arbitrary"), 48),
        name="nsa_attention",
    )(q, gates, slopes, cmp2, cmp2, kv, kv, kv, kv, expand)


def _gate_weight(w_in):
    D = w_in.shape[0]
    wg = w_in[:, N_HEADS * HEAD_DIM:].reshape(D, N_BRANCH, N_KV, GROUP).transpose(0, 2, 1, 3)
    wg = wg.reshape(D, N_KV, N_BRANCH * GROUP)
    wg = jnp.pad(wg, ((0, 0), (0, 0), (0, 128 - N_BRANCH * GROUP)))
    return wg.reshape(D, N_KV * 128).astype(BF16)


def _conv_ffn_layer(x, m, norm_gain, w_up, conv_w, conv_b, w_down):
    D = D_MODEL
    sh2, sc2, g2 = m[:, 3 * D:4 * D], m[:, 4 * D:5 * D], m[:, 5 * D:6 * D]
    act = _ffn_up(x, norm_gain.reshape(1, D), sc2, sh2, w_up.astype(BF16), conv_w, conv_b)
    return _mm_res(act, w_down.astype(BF16), x, g2)


def kernel(x, c, mod_w, mod_b, norm_gain, ffn_w_up, ffn_conv_w, ffn_conv_b, ffn_w_down, sgu_w_in, sgu_v_gain, sgu_w_s, sgu_b_s, sgu_w_out, kv_gain, kv_mod_w, kv_mod_b, w_kv, cmp_pos, cmp_w1, cmp_w2, k_gain, nsa_w_in, nsa_q_gain, nsa_w_o):
    B, T, D = x.shape
    assert B == 1 and D == D_MODEL and T % 1024 == 0
    xs = x[0]
    c_col = c.reshape(D, 1)
    mods = _adaln(c_col, mod_w, mod_b)
    kv_mod = _adaln(c_col, kv_mod_w[None], kv_mod_b[None])[0]

    h = jnp.arange(1, N_HEADS + 1, dtype=F32)
    slopes = (2.0 ** (-8.0 * h / N_HEADS)).reshape(N_KV, GROUP)
    slopes = jnp.repeat(slopes, Q_BLOCK, axis=1)[:, :, None]
    blk_of_key = jnp.arange(T, dtype=jnp.int32) // SLC_LEN
    expand = (blk_of_key[None, :] == jnp.arange(T // SLC_LEN, dtype=jnp.int32)[:, None]).astype(BF16)

    kv = cmp = None
    for layer in range(DEPTH):
        m = mods[layer]
        sh1, sc1, g1 = m[:, 0:D], m[:, D:2 * D], m[:, 2 * D:3 * D]
        gain1 = norm_gain[layer, 0].reshape(1, D)
        if layer == N_A_LAYERS:
            key_gain = jnp.ones((N_KV_TENSORS, 1, HEAD_DIM), F32).at[2, 0].set(k_gain[1]).at[4, 0].set(k_gain[2])
            kv = _kv_proj(xs, kv_gain.reshape(1, D), kv_mod[:, D:], kv_mod[:, :D], w_kv.astype(BF16), key_gain)
            cmp = _compress(kv, cmp_pos, cmp_w1, cmp_w2, k_gain[0])
        if layer < N_A_LAYERS:
            a = layer
            uv, ss = _sgu_in(xs, gain1, sc1, sh1, sgu_w_in[a].astype(BF16))
            y = _sgu_gate(uv, ss, sgu_v_gain[a], sgu_w_s[a], sgu_b_s[a])
            xs = _mm_res(y, sgu_w_out[a].astype(BF16), xs, g1)
        else:
            bl = layer - N_A_LAYERS
            w_in = nsa_w_in[bl]
            q, gates = _q_proj(xs, gain1, sc1, sh1, w_in[:, :N_HEADS * HEAD_DIM].astype(BF16),
                               _gate_weight(w_in), nsa_q_gain[bl])
            o = _attention(q, gates, slopes, cmp, kv, expand)
            xs = _mm_res(o, nsa_w_o[bl].astype(BF16), xs, g1)
        xs = _conv_ffn_layer(xs, m, norm_gain[layer, 1], ffn_w_up[layer], ffn_conv_w[layer],
                             ffn_conv_b[layer], ffn_w_down[layer])
    return xs[None]
```

```python
import functools

import jax
import jax.numpy as jnp
from jax import lax
from jax.experimental import pallas as pl
from jax.experimental.pallas import tpu as pltpu

F32 = jnp.float32
BF16 = jnp.bfloat16

D_MODEL = 2048
DEPTH = 4
N_A_LAYERS = 2
EPS = 1e-6
N_MOD = 6

CHUNK = 128
SGU_WIDTH = 2 * D_MODEL
SGU_GROUPS = 16
SGU_GROUP_DIM = SGU_WIDTH // SGU_GROUPS

D_FF = 11 * D_MODEL // 4
CONV_W = 3
CONV_HALO = 16

N_HEADS = 16
HEAD_DIM = 128
N_KV = 4
GROUP = 4
N_BRANCH = 3
N_KV_TENSORS = 6
CMP_LEN = 32
CMP_STRIDE = 16
SLC_LEN = 64
SLC_TOPK = 16
N_FORCED_LOCAL = 2
WINDOW = 512
Q_BLOCK = 128
NEG_INF = -1e30
FORCE_BONUS = 1e4
QK_SCALE = HEAD_DIM ** -0.5

KV_TILE = 512
WIN_KEYS = WINDOW + Q_BLOCK
GATE_ROWS = 16
AUG_BLOCKS = 16
AUG_OFFSET_BASE = 16

MIB = 1024 * 1024


def _params(sem, vmem_mib):
    return pltpu.CompilerParams(dimension_semantics=sem, vmem_limit_bytes=vmem_mib * MIB)


def _sigmoid(x):
    return 1.0 / (1.0 + jnp.exp(-x))


def _gelu_tanh(x):
    return 0.5 * x * (1.0 + jnp.tanh(0.7978845608028654 * (x + 0.044715 * (x * x * x))))


def _modnorm(x, gain, scale, shift):
    r = lax.rsqrt(jnp.mean(x * x, axis=-1, keepdims=True) + EPS)
    return (x * r) * gain * (1.0 + scale) + shift


def _head_rms(z, gain):
    return z * lax.rsqrt(jnp.mean(z * z, axis=-1, keepdims=True) + EPS) * gain


def _dot(a, b):
    return jnp.dot(a, b, preferred_element_type=F32)


def _dot_nt(a, b):
    return lax.dot_general(a, b, (((1,), (1,)), ((), ())), preferred_element_type=F32)


def _adaln_kernel(c_ref, w_ref, b_ref, o_ref):
    c = c_ref[...]
    s = c * _sigmoid(c)
    o_ref[0] = jnp.sum(s * w_ref[0], axis=0, keepdims=True) + b_ref[0]


def _adaln(c_col, w, b, tn=1024):
    L, D, N = w.shape
    return pl.pallas_call(
        _adaln_kernel,
        grid=(L, N // tn),
        in_specs=[
            pl.BlockSpec((D, 1), lambda l, j: (0, 0)),
            pl.BlockSpec((1, D, tn), lambda l, j: (l, 0, j)),
            pl.BlockSpec((1, 1, tn), lambda l, j: (l, 0, j)),
        ],
        out_specs=pl.BlockSpec((1, 1, tn), lambda l, j: (l, 0, j)),
        out_shape=jax.ShapeDtypeStruct((L, 1, N), F32),
        compiler_params=_params(("arbitrary", "arbitrary"), 40),
        name="adaln",
    )(c_col, w, b.reshape(L, 1, N))


def _sgu_in_kernel(x_ref, g_ref, sc_ref, sh_ref, w_ref, uv_ref, ss_ref, h_scr, *, n_u_tiles):
    j = pl.program_id(1)

    @pl.when(j == 0)
    def _():
        h_scr[...] = _modnorm(x_ref[...], g_ref[...], sc_ref[...], sh_ref[...]).astype(BF16)
        ss_ref[...] = jnp.zeros_like(ss_ref)

    y = _gelu_tanh(_dot(h_scr[...], w_ref[...]))
    uv_ref[...] = y.astype(BF16)

    @pl.when(j >= n_u_tiles)
    def _():
        ss_ref[...] += jnp.sum(y * y, axis=-1, keepdims=True)


def _sgu_in(x, gain, scale, shift, w, tm=1024, tn=512):
    T, D = x.shape
    N = w.shape[1]
    vec = pl.BlockSpec((1, D), lambda i, j: (0, 0))
    return pl.pallas_call(
        functools.partial(_sgu_in_kernel, n_u_tiles=SGU_WIDTH // tn),
        grid=(T // tm, N // tn),
        in_specs=[pl.BlockSpec((tm, D), lambda i, j: (i, 0)), vec, vec, vec,
                  pl.BlockSpec((D, tn), lambda i, j: (0, j))],
        out_specs=[pl.BlockSpec((tm, tn), lambda i, j: (i, j)),
                   pl.BlockSpec((tm, 1), lambda i, j: (i, 0))],
        out_shape=[jax.ShapeDtypeStruct((T, N), BF16), jax.ShapeDtypeStruct((T, 1), F32)],
        scratch_shapes=[pltpu.VMEM((tm, D), BF16)],
        compiler_params=_params(("arbitrary", "arbitrary"), 48),
        name="sgu_in",
    )(x, gain, scale, shift, w)


def _sgu_gate_kernel(u_ref, v_ref, ss_ref, vg_ref, w_ref, b_ref, o_ref, *, n_chunks):
    row = lax.broadcasted_iota(jnp.int32, (CHUNK, CHUNK), 0)
    col = lax.broadcasted_iota(jnp.int32, (CHUNK, CHUNK), 1)
    wc = jnp.where(row >= col, w_ref[0], 0.0).astype(BF16)
    r = lax.rsqrt(ss_ref[...] * (1.0 / SGU_WIDTH) + EPS)
    v = (v_ref[...].astype(F32) * r * vg_ref[...]).astype(BF16)
    b = b_ref[0]
    for n in range(n_chunks):
        rows = slice(n * CHUNK, (n + 1) * CHUNK)
        sv = _dot(wc, v[rows]) + b
        o_ref[rows, :] = (u_ref[rows, :].astype(F32) * sv).astype(BF16)


def _sgu_gate(uv, ss, v_gain, w_s, b_s, tc=512):
    T = uv.shape[0]
    gd = SGU_GROUP_DIM
    return pl.pallas_call(
        functools.partial(_sgu_gate_kernel, n_chunks=tc // CHUNK),
        grid=(SGU_GROUPS, T // tc),
        in_specs=[
            pl.BlockSpec((tc, gd), lambda g, i: (i, g)),
            pl.BlockSpec((tc, gd), lambda g, i: (i, SGU_GROUPS + g)),
            pl.BlockSpec((tc, 1), lambda g, i: (i, 0)),
            pl.BlockSpec((1, gd), lambda g, i: (0, g)),
            pl.BlockSpec((1, CHUNK, CHUNK), lambda g, i: (g, 0, 0)),
            pl.BlockSpec((1, CHUNK, 1), lambda g, i: (g, 0, 0)),
        ],
        out_specs=pl.BlockSpec((tc, gd), lambda g, i: (i, g)),
        out_shape=jax.ShapeDtypeStruct((T, SGU_WIDTH), BF16),
        compiler_params=_params(("arbitrary", "arbitrary"), 32),
        name="sgu_gate",
    )(uv, uv, ss, v_gain.reshape(1, SGU_WIDTH), w_s, b_s.reshape(SGU_GROUPS, CHUNK, 1))


def _mm_res_kernel(a_ref, w_ref, x_ref, g_ref, o_ref):
    o_ref[...] = x_ref[...] + g_ref[...] * _dot(a_ref[...], w_ref[...])


def _mm_res(a, w, x, gate, tm=512, tn=512):
    T, K = a.shape
    N = w.shape[1]
    return pl.pallas_call(
        _mm_res_kernel,
        grid=(T // tm, N // tn),
        in_specs=[pl.BlockSpec((tm, K), lambda i, j: (i, 0)),
                  pl.BlockSpec((K, tn), lambda i, j: (0, j)),
                  pl.BlockSpec((tm, tn), lambda i, j: (i, j)),
                  pl.BlockSpec((1, tn), lambda i, j: (0, j))],
        out_specs=pl.BlockSpec((tm, tn), lambda i, j: (i, j)),
        out_shape=jax.ShapeDtypeStruct((T, N), F32),
        compiler_params=_params(("arbitrary", "arbitrary"), 48),
        name="mm_res",
    )(a, w, x, gate)


def _ffn_up_kernel(xh_ref, x_ref, g_ref, sc_ref, sh_ref, wa_ref, wv_ref, cwa_ref, cwv_ref,
                   cba_ref, cbv_ref, o_ref, h_scr, za_scr, zv_scr, *, tm):
    i = pl.program_id(0)
    j = pl.program_id(1)

    @pl.when(j == 0)
    def _():
        g, sc, sh = g_ref[...], sc_ref[...], sh_ref[...]
        h_scr[CONV_HALO:, :] = _modnorm(x_ref[...], g, sc, sh).astype(BF16)
        halo = _modnorm(xh_ref[...], g, sc, sh)
        h_scr[:CONV_HALO, :] = jnp.where(i > 0, halo, 0.0).astype(BF16)

    h = h_scr[...]
    za_scr[...] = _dot(h, wa_ref[...])
    zv_scr[...] = _dot(h, wv_ref[...])

    def conv(z_scr, cw_ref, cb_ref):
        acc = cb_ref[...] + cw_ref[0:1, :] * z_scr[pl.ds(CONV_HALO - 2, tm), :]
        acc = acc + cw_ref[1:2, :] * z_scr[pl.ds(CONV_HALO - 1, tm), :]
        return acc + cw_ref[2:3, :] * z_scr[pl.ds(CONV_HALO, tm), :]

    a = conv(za_scr, cwa_ref, cba_ref)
    v = conv(zv_scr, cwv_ref, cbv_ref)
    o_ref[...] = (a * _sigmoid(a) * v).astype(BF16)


def _ffn_up(x, gain, scale, shift, w_up, conv_w, conv_b, tm=1024, tn=512):
    T, D = x.shape
    nj = D_FF // tn
    hb = tm // CONV_HALO
    vec = pl.BlockSpec((1, D), lambda i, j: (0, 0))
    conv_b = conv_b.reshape(1, 2 * D_FF)
    return pl.pallas_call(
        functools.partial(_ffn_up_kernel, tm=tm),
        grid=(T // tm, nj),
        in_specs=[
            pl.BlockSpec((CONV_HALO, D), lambda i, j: (jnp.maximum(i * hb - 1, 0), 0)),
            pl.BlockSpec((tm, D), lambda i, j: (i, 0)),
            vec, vec, vec,
            pl.BlockSpec((D, tn), lambda i, j: (0, j)),
            pl.BlockSpec((D, tn), lambda i, j: (0, j + nj)),
            pl.BlockSpec((CONV_W, tn), lambda i, j: (0, j)),
            pl.BlockSpec((CONV_W, tn), lambda i, j: (0, j + nj)),
            pl.BlockSpec((1, tn), lambda i, j: (0, j)),
            pl.BlockSpec((1, tn), lambda i, j: (0, j + nj)),
        ],
        out_specs=pl.BlockSpec((tm, tn), lambda i, j: (i, j)),
        out_shape=jax.ShapeDtypeStruct((T, D_FF), BF16),
        scratch_shapes=[pltpu.VMEM((tm + CONV_HALO, D), BF16),
                        pltpu.VMEM((tm + CONV_HALO, tn), F32),
                        pltpu.VMEM((tm + CONV_HALO, tn), F32)],
        compiler_params=_params(("arbitrary", "arbitrary"), 52),
        name="ffn_up",
    )(x, x, gain, scale, shift, w_up, w_up, conv_w, conv_w, conv_b, conv_b)


def _kv_kernel(x_ref, g_ref, sc_ref, sh_ref, w_ref, kg_ref, o_ref, vt_ref, h_scr):
    j = pl.program_id(1)

    @pl.when(j == 0)
    def _():
        h_scr[...] = _modnorm(x_ref[...], g_ref[...], sc_ref[...], sh_ref[...]).astype(BF16)

    z = _dot(h_scr[...], w_ref[...])
    is_key_normed = (j == 2) | (j == 4)
    gain = kg_ref[0]
    for g in range(N_KV):
        zg = z[:, g * HEAD_DIM:(g + 1) * HEAD_DIM]
        o_ref[g] = jnp.where(is_key_normed, _head_rms(zg, gain), zg).astype(BF16)

    @pl.when((j == 3) | (j == 5))
    def _():
        for g in range(N_KV):
            vt_ref[0, g] = z[:, g * HEAD_DIM:(g + 1) * HEAD_DIM].T.astype(BF16)


def _kv_proj(x, gain, scale, shift, w, key_gain, tm=1024):
    T, D = x.shape
    tn = N_KV * HEAD_DIM
    vec = pl.BlockSpec((1, D), lambda i, j: (0, 0))
    return pl.pallas_call(
        _kv_kernel,
        grid=(T // tm, N_KV_TENSORS),
        in_specs=[pl.BlockSpec((tm, D), lambda i, j: (i, 0)), vec, vec, vec,
                  pl.BlockSpec((D, tn), lambda i, j: (0, j)),
                  pl.BlockSpec((1, 1, HEAD_DIM), lambda i, j: (j, 0, 0))],
        out_specs=[pl.BlockSpec((N_KV, tm, HEAD_DIM), lambda i, j: (j, i, 0)),
                   pl.BlockSpec((1, N_KV, HEAD_DIM, tm), lambda i, j: (j // 4, 0, 0, i))],
        out_shape=[jax.ShapeDtypeStruct((N_KV_TENSORS * N_KV, T, HEAD_DIM), BF16),
                   jax.ShapeDtypeStruct((2, N_KV, HEAD_DIM, T), BF16)],
        scratch_shapes=[pltpu.VMEM((tm, D), BF16)],
        compiler_params=_params(("arbitrary", "arbitrary"), 48),
        name="kv_proj",
    )(x, gain, scale, shift, w, key_gain)


def _cmp_kernel(seg_ref, pos_ref, w1_ref, w2_ref, kg_ref, o_ref, b_scr, *, n_seg, is_key):
    seg = seg_ref[0]
    w1 = w1_ref[0]
    half = CMP_STRIDE * HEAD_DIM
    first = _dot(seg, w1[:half].astype(BF16))
    second = _dot(seg, w1[half:].astype(BF16))
    pos_bias = jnp.sum(pos_ref[0] * w1, axis=0, keepdims=True)
    b_scr[pl.ds(0, n_seg), :] = second
    b_scr[pl.ds(n_seg, 8), :] = jnp.zeros((8, HEAD_DIM), F32)
    pre = first + b_scr[pl.ds(1, n_seg), :] + pos_bias
    out = _dot(_gelu_tanh(pre).astype(BF16), w2_ref[0].astype(BF16))
    if is_key:
        o_ref[0] = _head_rms(out, kg_ref[...]).astype(BF16)
    else:
        b_scr[pl.ds(0, n_seg), :] = out
        o_ref[0] = b_scr[pl.ds(0, n_seg), :].T.astype(BF16)


def _compress(kv, which, cmp_pos, cmp_w1, cmp_w2, k_gain_cmp):
    T = kv.shape[1]
    n_seg = T // CMP_STRIDE
    seg_w = CMP_STRIDE * HEAD_DIM
    segs = kv.reshape(kv.shape[0], n_seg, seg_w)
    flat = CMP_LEN * HEAD_DIM
    is_key = which == 0
    out_block = (1, n_seg, HEAD_DIM) if is_key else (1, HEAD_DIM, n_seg)
    return pl.pallas_call(
        functools.partial(_cmp_kernel, n_seg=n_seg, is_key=is_key),
        grid=(N_KV,),
        in_specs=[pl.BlockSpec((1, n_seg, seg_w), lambda g: (which * N_KV + g, 0, 0)),
                  pl.BlockSpec((1, flat, 1), lambda g: (which, 0, 0)),
                  pl.BlockSpec((1, flat, HEAD_DIM), lambda g: (which, 0, 0)),
                  pl.BlockSpec((1, HEAD_DIM, HEAD_DIM), lambda g: (which, 0, 0)),
                  pl.BlockSpec((1, HEAD_DIM), lambda g: (0, 0))],
        out_specs=pl.BlockSpec(out_block, lambda g: (g, 0, 0)),
        out_shape=jax.ShapeDtypeStruct((N_KV,) + out_block[1:], BF16),
        scratch_shapes=[pltpu.VMEM((n_seg + 8, HEAD_DIM), F32)],
        compiler_params=_params(("arbitrary",), 40),
        name="compress",
    )(segs, cmp_pos.reshape(2, flat, 1), cmp_w1, cmp_w2, k_gain_cmp.reshape(1, HEAD_DIM))


def _q_kernel(x_ref, g_ref, sc_ref, sh_ref, wq_ref, wg_ref, qg_ref, q_ref, gt_ref, h_scr):
    j = pl.program_id(1)

    @pl.when(j == 0)
    def _():
        h_scr[...] = _modnorm(x_ref[...], g_ref[...], sc_ref[...], sh_ref[...]).astype(BF16)
        gates = _sigmoid(_dot(h_scr[...], wg_ref[...]))
        for g in range(N_KV):
            gt_ref[g] = gates[:, g * 128:(g + 1) * 128].T[:GATE_ROWS]

    z = _dot(h_scr[...], wq_ref[...])
    gain = qg_ref[...] * QK_SCALE
    for r in range(GROUP):
        cols = slice(r * HEAD_DIM, (r + 1) * HEAD_DIM)
        q_ref[r] = _head_rms(z[:, cols], gain).T.astype(BF16)


def _q_proj(x, gain, scale, shift, w_q, w_gate, q_gain, tm=1024):
    T, D = x.shape
    tn = GROUP * HEAD_DIM
    vec = pl.BlockSpec((1, D), lambda i, j: (0, 0))
    ng = w_gate.shape[1]
    return pl.pallas_call(
        _q_kernel,
        grid=(T // tm, N_KV),
        in_specs=[pl.BlockSpec((tm, D), lambda i, j: (i, 0)), vec, vec, vec,
                  pl.BlockSpec((D, tn), lambda i, j: (0, j)),
                  pl.BlockSpec((D, ng), lambda i, j: (0, 0)),
                  pl.BlockSpec((1, HEAD_DIM), lambda i, j: (0, 0))],
        out_specs=[pl.BlockSpec((GROUP, HEAD_DIM, tm), lambda i, j: (j, 0, i)),
                   pl.BlockSpec((N_KV, GATE_ROWS, tm), lambda i, j: (0, 0, i))],
        out_shape=[jax.ShapeDtypeStruct((N_HEADS, HEAD_DIM, T), BF16),
                   jax.ShapeDtypeStruct((N_KV, GATE_ROWS, T), F32)],
        scratch_shapes=[pltpu.VMEM((tm, D), BF16)],
        compiler_params=_params(("arbitrary", "arbitrary"), 48),
        name="q_proj",
    )(x, gain, scale, shift, w_q, w_gate, q_gain.reshape(1, HEAD_DIM))


def _masked_softmax(s, valid):
    s = jnp.where(valid, s, NEG_INF)
    m = jnp.max(s, axis=-1, keepdims=True)
    e = jnp.where(valid, jnp.exp(s - m), 0.0)
    return e / jnp.maximum(jnp.sum(e, axis=-1, keepdims=True), 1e-30)


def _split3_bf16(x):
    hi = x.astype(BF16)
    r1 = x - hi.astype(F32)
    mid = r1.astype(BF16)
    lo = (r1 - mid.astype(F32)).astype(BF16)
    return hi, mid, lo


def _attn_kernel(q_ref, gt_ref, sl_ref, kc_ref, vc_ref, ks_ref, vs_ref, kw_ref, vw_ref,
                 o_ref, imp_scr, m_scr, l_scr, acc_scr, *, n_cmp, n_slc):
    qb = pl.program_id(1)
    t0 = qb * Q_BLOCK
    rows = GROUP * Q_BLOCK
    slope = sl_ref[0]
    q = jnp.concatenate([q_ref[:, r * HEAD_DIM:(r + 1) * HEAD_DIM] for r in range(GROUP)], axis=0)

    def token_of_row(shape):
        return t0 + lax.broadcasted_iota(jnp.int32, shape, 0) % Q_BLOCK

    s = _dot_nt(q, kc_ref[0].astype(BF16))
    ci = lax.broadcasted_iota(jnp.int32, (rows, n_cmp), 1)
    d = (token_of_row((rows, n_cmp)) - (ci * CMP_STRIDE + (CMP_LEN - 1))).astype(F32)
    p = _masked_softmax(s - slope * d, d >= 0)
    o_cmp = _dot(p.astype(BF16), vc_ref[0].astype(BF16))

    p_group = p[0:Q_BLOCK]
    for r in range(1, GROUP):
        p_group = p_group + p[r * Q_BLOCK:(r + 1) * Q_BLOCK]
    oj = lax.broadcasted_iota(jnp.int32, (n_slc, n_cmp), 0) * (SLC_LEN // CMP_STRIDE)
    oi = lax.broadcasted_iota(jnp.int32, (n_slc, n_cmp), 1)
    overlap = ((oi > oj - CMP_LEN // CMP_STRIDE) & (oi < oj + SLC_LEN // CMP_STRIDE)).astype(BF16)
    imp = sum(_dot_nt(overlap, part) for part in _split3_bf16(p_group))
    blk = lax.broadcasted_iota(jnp.int32, (n_slc, Q_BLOCK), 0)
    cur = (t0 + lax.broadcasted_iota(jnp.int32, (n_slc, Q_BLOCK), 1)) // SLC_LEN
    forced = (blk == 0) | ((blk <= cur) & (blk > cur - N_FORCED_LOCAL))
    imp = jnp.where(forced, imp + FORCE_BONUS, imp)
    imp = jnp.where(blk <= cur, imp, NEG_INF)

    imp_scr[...] = imp
    n_grp = n_slc // 8
    imp_rows = [imp[8 * v:8 * v + 8] for v in range(n_grp)]
    sub = lax.broadcasted_iota(jnp.int32, (8, Q_BLOCK), 0)
    rank = [jnp.zeros((8, Q_BLOCK), jnp.int32) for _ in range(n_grp)]
    for jp in range(n_slc):
        other = imp_scr[jp:jp + 1, :]
        for v in range(n_grp):
            if 8 * v > jp:
                beats = other >= imp_rows[v]
            elif 8 * v + 7 < jp:
                beats = other > imp_rows[v]
            else:
                beats = (other > imp_rows[v]) | ((other == imp_rows[v]) & (sub > jp - 8 * v))
            rank[v] = rank[v] + jnp.where(beats, 1, 0)
    rank = jnp.concatenate(rank, axis=0)
    keep = (rank < SLC_TOPK) & (blk <= cur)
    not_kept = jnp.where(keep, 0.0, NEG_INF).T
    if n_slc < 128:
        not_kept = jnp.concatenate([not_kept, jnp.zeros((Q_BLOCK, 128 - n_slc), F32)], axis=1)
    not_kept = not_kept.astype(BF16)
    q_aug = jnp.concatenate(
        [jnp.concatenate([q_ref[:, r * HEAD_DIM:(r + 1) * HEAD_DIM], not_kept], axis=1) for r in range(GROUP)], axis=0)

    m_scr[...] = jnp.full(m_scr.shape, NEG_INF, F32)
    l_scr[...] = jnp.zeros(l_scr.shape, F32)
    acc_scr[...] = jnp.zeros(acc_scr.shape, F32)
    kcol = lax.broadcasted_iota(jnp.int32, (rows, KV_TILE), 1)
    col_bias = slope * kcol.astype(F32)
    t_rel = lax.broadcasted_iota(jnp.int32, (rows, 1), 0) % Q_BLOCK
    n_full = t0 // KV_TILE

    def key_tile(k0, causal):
        s = _dot_nt(q_aug, ks_ref[0, pl.ds(k0, KV_TILE), :]) + col_bias
        if causal:
            s = jnp.where(t_rel + (t0 - k0) >= kcol, s, NEG_INF)
        row_term = slope * (t_rel + (t0 - k0)).astype(F32)
        m_old = m_scr[...]
        m_new = jnp.maximum(m_old, jnp.max(s, axis=-1, keepdims=True) - row_term)
        e = jnp.exp(s - (m_new + row_term))
        alpha = jnp.exp(m_old - m_new)
        l_scr[...] = alpha * l_scr[...] + jnp.sum(e, axis=-1, keepdims=True)
        acc_scr[...] = alpha * acc_scr[...] + _dot(e.astype(BF16), vs_ref[0, pl.ds(k0, KV_TILE), :])
        m_scr[...] = m_new

    def full_tile(kt, carry):
        key_tile(pl.multiple_of(kt * KV_TILE, KV_TILE), causal=False)
        return carry

    lax.fori_loop(0, n_full, full_tile, 0)
    key_tile(pl.multiple_of(n_full * KV_TILE, KV_TILE), causal=True)
    o_slc = acc_scr[...] / l_scr[...]

    w0 = pl.multiple_of(jnp.maximum(t0 - WINDOW, 0), Q_BLOCK)
    s = _dot_nt(q, kw_ref[0, pl.ds(w0, WIN_KEYS), :])
    wcol = lax.broadcasted_iota(jnp.int32, (rows, WIN_KEYS), 1)
    d = (token_of_row((rows, WIN_KEYS)) - (w0 + wcol)).astype(F32)
    p = _masked_softmax(s - slope * d, (d >= 0) & (d < WINDOW))
    o_win = _dot(p.astype(BF16), vw_ref[0, pl.ds(w0, WIN_KEYS), :])

    gates = gt_ref[...]
    for r in range(GROUP):
        hr = slice(r * Q_BLOCK, (r + 1) * Q_BLOCK)
        o = (gates[:, r:r + 1] * o_cmp[hr]
             + gates[:, GROUP + r:GROUP + r + 1] * o_slc[hr]
             + gates[:, 2 * GROUP + r:2 * GROUP + r + 1] * o_win[hr])
        o_ref[:, r * HEAD_DIM:(r + 1) * HEAD_DIM] = o.astype(BF16)


def _attention(q, gates, slopes, cmp, kv, k_slc_aug):
    T = q.shape[0]
    n_cmp = T // CMP_STRIDE
    n_slc = T // SLC_LEN
    rows = GROUP * Q_BLOCK
    hw = GROUP * HEAD_DIM

    def kv_spec(tensor):
        return pl.BlockSpec((1, T, HEAD_DIM), lambda g, qb: (tensor * N_KV + g, 0, 0))

    def cmp_spec(which):
        return pl.BlockSpec((1, n_cmp, HEAD_DIM), lambda g, qb: (which * N_KV + g, 0, 0))

    cmp2 = cmp.reshape(2 * N_KV, n_cmp, HEAD_DIM)
    return pl.pallas_call(
        functools.partial(_attn_kernel, n_cmp=n_cmp, n_slc=n_slc),
        grid=(N_KV, T // Q_BLOCK),
        in_specs=[pl.BlockSpec((Q_BLOCK, hw), lambda g, qb: (qb, g)),
                  pl.BlockSpec((Q_BLOCK, 128), lambda g, qb: (qb, g)),
                  pl.BlockSpec((1, rows, 1), lambda g, qb: (g, 0, 0)),
                  cmp_spec(0), cmp_spec(1),
                  pl.BlockSpec((1, T, 2 * HEAD_DIM), lambda g, qb: (g, 0, 0)),
                  kv_spec(3), kv_spec(4), kv_spec(5)],
        out_specs=pl.BlockSpec((Q_BLOCK, hw), lambda g, qb: (qb, g)),
        out_shape=jax.ShapeDtypeStruct((T, N_HEADS * HEAD_DIM), BF16),
        scratch_shapes=[pltpu.VMEM((n_slc, Q_BLOCK), F32),
                        pltpu.VMEM((rows, 1), F32),
                        pltpu.VMEM((rows, 1), F32),
                        pltpu.VMEM((rows, HEAD_DIM), F32)],
        compiler_params=_params(("arbitrary", "arbitrary"), 48),
        name="nsa_attention",
    )(q, gates, slopes, cmp2, cmp2, k_slc_aug, kv, kv, kv)


def _softmax_over_keys(s, valid):
    s = jnp.where(valid, s, NEG_INF)
    m = jnp.max(s, axis=0, keepdims=True)
    e = jnp.where(valid, jnp.exp(s - m), 0.0)
    return e / jnp.maximum(jnp.sum(e, axis=0, keepdims=True), 1e-30)


def _attn_t_kernel(q_ref, gt_ref, sl_ref, qx_ref, kc_ref, vc_ref, ks_ref, vs_ref, kw_ref, vw_ref,
                   o_ref, imp_scr, nk_scr, qa_scr, m_scr, l_scr, acc_scr, *, n_cmp, n_slc):
    qb = pl.program_id(1)
    t0 = qb * Q_BLOCK
    lanes = GROUP * Q_BLOCK
    slope = sl_ref[0]
    q_t = jnp.concatenate([q_ref[r] for r in range(GROUP)], axis=1)
    t_rel = lax.broadcasted_iota(jnp.int32, (1, lanes), 1) % Q_BLOCK

    s = _dot(kc_ref[0], q_t)
    ci = lax.broadcasted_iota(jnp.int32, (n_cmp, lanes), 0)
    d = ((t0 + t_rel) - (ci * CMP_STRIDE + (CMP_LEN - 1))).astype(F32)
    p = _softmax_over_keys(s - slope * d, d >= 0)
    o_cmp = _dot(vc_ref[0], p.astype(BF16))

    p_group = p[:, 0:Q_BLOCK]
    for r in range(1, GROUP):
        p_group = p_group + p[:, r * Q_BLOCK:(r + 1) * Q_BLOCK]
    oj = lax.broadcasted_iota(jnp.int32, (n_slc, n_cmp), 0) * (SLC_LEN // CMP_STRIDE)
    oi = lax.broadcasted_iota(jnp.int32, (n_slc, n_cmp), 1)
    overlap = ((oi > oj - CMP_LEN // CMP_STRIDE) & (oi < oj + SLC_LEN // CMP_STRIDE)).astype(BF16)
    imp = sum(_dot(overlap, part) for part in _split3_bf16(p_group))
    blk = lax.broadcasted_iota(jnp.int32, (n_slc, Q_BLOCK), 0)
    cur = (t0 + lax.broadcasted_iota(jnp.int32, (n_slc, Q_BLOCK), 1)) // SLC_LEN
    forced = (blk == 0) | ((blk <= cur) & (blk > cur - N_FORCED_LOCAL))
    imp = jnp.where(forced, imp + FORCE_BONUS, imp)
    imp = jnp.where(blk <= cur, imp, NEG_INF)

    imp_scr[...] = imp
    n_grp = n_slc // 8
    imp_rows = [imp[8 * v:8 * v + 8] for v in range(n_grp)]
    sub = lax.broadcasted_iota(jnp.int32, (8, Q_BLOCK), 0)
    rank = [jnp.zeros((8, Q_BLOCK), jnp.int32) for _ in range(n_grp)]
    for jp in range(n_slc):
        other = imp_scr[jp:jp + 1, :]
        for v in range(n_grp):
            if 8 * v > jp:
                beats = other >= imp_rows[v]
            elif 8 * v + 7 < jp:
                beats = other > imp_rows[v]
            else:
                beats = (other > imp_rows[v]) | ((other == imp_rows[v]) & (sub > jp - 8 * v))
            rank[v] = rank[v] + jnp.where(beats, 1, 0)
    rank = jnp.concatenate(rank, axis=0)
    keep = (rank < SLC_TOPK) & (blk <= cur)
    not_kept = jnp.where(keep, 0.0, NEG_INF).astype(BF16)
    for r in range(GROUP):
        nk_scr[:, r * Q_BLOCK:(r + 1) * Q_BLOCK] = not_kept

    qa_scr[0:HEAD_DIM, :] = q_t
    qa_scr[HEAD_DIM + AUG_BLOCKS:, :] = qx_ref[0]

    m_scr[...] = jnp.full(m_scr.shape, NEG_INF, F32)
    l_scr[...] = jnp.zeros(l_scr.shape, F32)
    acc_scr[...] = jnp.zeros(acc_scr.shape, F32)
    krow = lax.broadcasted_iota(jnp.int32, (KV_TILE, lanes), 0)
    n_full = t0 // KV_TILE

    def scores(kt):
        k0 = pl.multiple_of(kt * KV_TILE, KV_TILE)
        first_blk = pl.multiple_of((kt * (KV_TILE // SLC_LEN) // AUG_BLOCKS) * AUG_BLOCKS, AUG_BLOCKS)
        qa_scr[HEAD_DIM:HEAD_DIM + AUG_BLOCKS, :] = nk_scr[pl.ds(first_blk, AUG_BLOCKS), :]
        return _dot(ks_ref[0, pl.ds(k0, KV_TILE), :], qa_scr[...])

    def accumulate(s, kt, causal):
        k0 = pl.multiple_of(kt * KV_TILE, KV_TILE)
        off = t_rel + (t0 - k0)
        if causal:
            s = jnp.where(krow <= off, s, NEG_INF)
        col_term = slope * off.astype(F32)
        m_old = m_scr[...]
        m_new = jnp.maximum(m_old, jnp.max(s, axis=0, keepdims=True) - col_term)
        e = jnp.exp(s - (m_new + col_term))
        alpha = jnp.exp(m_old - m_new)
        l_scr[...] = alpha * l_scr[...] + jnp.sum(e, axis=0, keepdims=True)
        acc_scr[...] = alpha * acc_scr[...] + _dot(vs_ref[0, :, pl.ds(k0, KV_TILE)], e.astype(BF16))
        m_scr[...] = m_new

    def full_tile(kt, carry):
        accumulate(scores(kt), kt, causal=False)
        return carry

    lax.fori_loop(0, n_full, full_tile, 0)
    accumulate(scores(n_full), n_full, causal=True)
    o_slc = acc_scr[...] / l_scr[...]

    w0 = pl.multiple_of(jnp.maximum(t0 - WINDOW, 0), Q_BLOCK)
    s = _dot(kw_ref[0, pl.ds(w0, WIN_KEYS), :], q_t)
    wrow = lax.broadcasted_iota(jnp.int32, (WIN_KEYS, lanes), 0)
    d = ((t0 - w0) + t_rel - wrow).astype(F32)
    p = _softmax_over_keys(s - slope * d, (d >= 0) & (d < WINDOW))
    o_win = _dot(vw_ref[0, :, pl.ds(w0, WIN_KEYS)], p.astype(BF16))

    gates = gt_ref[0]

    def gate_lanes(branch):
        return jnp.concatenate([gates[branch * GROUP + r:branch * GROUP + r + 1, :] for r in range(GROUP)], axis=1)

    o_t = gate_lanes(0) * o_cmp + gate_lanes(1) * o_slc + gate_lanes(2) * o_win
    for r in range(GROUP):
        o_ref[:, r * HEAD_DIM:(r + 1) * HEAD_DIM] = o_t[:, r * Q_BLOCK:(r + 1) * Q_BLOCK].T.astype(BF16)


def _attention_t(q_t, gates_t, slopes, q_extra, k_cmp, v_cmp_t, k_slc_aug, kv, v_t):
    T = q_t.shape[2]
    n_cmp = T // CMP_STRIDE
    n_slc = T // SLC_LEN
    lanes = GROUP * Q_BLOCK
    hw = GROUP * HEAD_DIM
    n_extra = q_extra.shape[1]

    def per_group(shape):
        return pl.BlockSpec((1,) + shape, lambda g, qb: (g, 0, 0))

    return pl.pallas_call(
        functools.partial(_attn_t_kernel, n_cmp=n_cmp, n_slc=n_slc),
        grid=(N_KV, T // Q_BLOCK),
        in_specs=[pl.BlockSpec((GROUP, HEAD_DIM, Q_BLOCK), lambda g, qb: (g, 0, qb)),
                  pl.BlockSpec((1, GATE_ROWS, Q_BLOCK), lambda g, qb: (g, 0, qb)),
                  per_group((1, lanes)),
                  per_group((n_extra, lanes)),
                  per_group((n_cmp, HEAD_DIM)),
                  per_group((HEAD_DIM, n_cmp)),
                  per_group((T, 2 * HEAD_DIM)),
                  per_group((HEAD_DIM, T)),
                  pl.BlockSpec((1, T, HEAD_DIM), lambda g, qb: (4 * N_KV + g, 0, 0)),
                  pl.BlockSpec((1, HEAD_DIM, T), lambda g, qb: (N_KV + g, 0, 0))],
        out_specs=pl.BlockSpec((Q_BLOCK, hw), lambda g, qb: (qb, g)),
        out_shape=jax.ShapeDtypeStruct((T, N_HEADS * HEAD_DIM), BF16),
        scratch_shapes=[pltpu.VMEM((n_slc, Q_BLOCK), F32),
                        pltpu.VMEM((n_slc, lanes), BF16),
                        pltpu.VMEM((2 * HEAD_DIM, lanes), BF16),
                        pltpu.VMEM((1, lanes), F32),
                        pltpu.VMEM((1, lanes), F32),
                        pltpu.VMEM((HEAD_DIM, lanes), F32)],
        compiler_params=_params(("arbitrary", "arbitrary"), 48),
        name="nsa_attention",
    )(q_t, gates_t, slopes, q_extra, k_cmp, v_cmp_t, k_slc_aug, v_t, kv, v_t)


def _gate_weight(w_in):
    D = w_in.shape[0]
    wg = w_in[:, N_HEADS * HEAD_DIM:].reshape(D, N_BRANCH, N_KV, GROUP).transpose(0, 2, 1, 3)
    wg = wg.reshape(D, N_KV, N_BRANCH * GROUP)
    wg = jnp.pad(wg, ((0, 0), (0, 0), (0, 128 - N_BRANCH * GROUP)))
    return wg.reshape(D, N_KV * 128).astype(BF16)


def _alibi_and_mask_columns(T):
    h = jnp.arange(1, N_HEADS + 1, dtype=F32)
    slopes = (2.0 ** (-8.0 * h / N_HEADS)).reshape(N_KV, GROUP)
    slopes = jnp.repeat(slopes, Q_BLOCK, axis=1)[:, None, :]
    hi, mid, lo = _split3_bf16(slopes)
    base = float(AUG_OFFSET_BASE)
    rows = [(hi.astype(F32) * base).astype(BF16), (mid.astype(F32) * base).astype(BF16),
            (lo.astype(F32) * base).astype(BF16), hi, mid, lo]
    n_pad = HEAD_DIM - AUG_BLOCKS - len(rows)
    q_extra = jnp.concatenate(rows + [jnp.zeros((N_KV, n_pad, GROUP * Q_BLOCK), BF16)], axis=1)

    kpos = jnp.arange(T, dtype=jnp.int32)
    one_hot = ((kpos // SLC_LEN) % AUG_BLOCKS)[:, None] == jnp.arange(AUG_BLOCKS, dtype=jnp.int32)[None, :]
    off = kpos % KV_TILE
    a = (off // AUG_OFFSET_BASE)[:, None]
    b = (off % AUG_OFFSET_BASE)[:, None]
    k_extra = jnp.concatenate([one_hot.astype(jnp.int32), a, a, a, b, b, b,
                               jnp.zeros((T, n_pad), jnp.int32)], axis=1).astype(BF16)
    return slopes, q_extra, k_extra


def _conv_ffn_layer(x, m, norm_gain, w_up, conv_w, conv_b, w_down):
    D = D_MODEL
    sh2, sc2, g2 = m[:, 3 * D:4 * D], m[:, 4 * D:5 * D], m[:, 5 * D:6 * D]
    act = _ffn_up(x, norm_gain.reshape(1, D), sc2, sh2, w_up.astype(BF16), conv_w, conv_b)
    return _mm_res(act, w_down.astype(BF16), x, g2)


def kernel(x, c, mod_w, mod_b, norm_gain, ffn_w_up, ffn_conv_w, ffn_conv_b, ffn_w_down, sgu_w_in, sgu_v_gain, sgu_w_s, sgu_b_s, sgu_w_out, kv_gain, kv_mod_w, kv_mod_b, w_kv, cmp_pos, cmp_w1, cmp_w2, k_gain, nsa_w_in, nsa_q_gain, nsa_w_o):
    B, T, D = x.shape
    assert B == 1 and D == D_MODEL and T % 1024 == 0
    xs = x[0]
    c_col = c.reshape(D, 1)
    mods = _adaln(c_col, mod_w, mod_b)
    kv_mod = _adaln(c_col, kv_mod_w[None], kv_mod_b[None])[0]

    slopes, q_extra, k_extra = _alibi_and_mask_columns(T)
    kv = v_t = k_cmp = v_cmp_t = k_slc_aug = None
    for layer in range(DEPTH):
        m = mods[layer]
        sh1, sc1, g1 = m[:, 0:D], m[:, D:2 * D], m[:, 2 * D:3 * D]
        gain1 = norm_gain[layer, 0].reshape(1, D)
        if layer == N_A_LAYERS:
            key_gain = jnp.ones((N_KV_TENSORS, 1, HEAD_DIM), F32).at[2, 0].set(k_gain[1]).at[4, 0].set(k_gain[2])
            kv, v_t = _kv_proj(xs, kv_gain.reshape(1, D), kv_mod[:, D:], kv_mod[:, :D], w_kv.astype(BF16), key_gain)
            v_t = v_t.reshape(2 * N_KV, HEAD_DIM, T)
            k_cmp = _compress(kv, 0, cmp_pos, cmp_w1, cmp_w2, k_gain[0])
            v_cmp_t = _compress(kv, 1, cmp_pos, cmp_w1, cmp_w2, k_gain[0])
            k_slc_aug = jnp.concatenate(
                [kv[2 * N_KV:3 * N_KV], jnp.broadcast_to(k_extra, (N_KV, T, HEAD_DIM))], axis=-1)
        if layer < N_A_LAYERS:
            a = layer
            uv, ss = _sgu_in(xs, gain1, sc1, sh1, sgu_w_in[a].astype(BF16))
            y = _sgu_gate(uv, ss, sgu_v_gain[a], sgu_w_s[a], sgu_b_s[a])
            xs = _mm_res(y, sgu_w_out[a].astype(BF16), xs, g1)
        else:
            bl = layer - N_A_LAYERS
            w_in = nsa_w_in[bl]
            q_t, gates_t = _q_proj(xs, gain1, sc1, sh1, w_in[:, :N_HEADS * HEAD_DIM].astype(BF16),
                                   _gate_weight(w_in), nsa_q_gain[bl])
            o = _attention_t(q_t, gates_t, slopes, q_extra, k_cmp, v_cmp_t, k_slc_aug, kv, v_t)
            xs = _mm_res(o, nsa_w_o[bl].astype(BF16), xs, g1)
        xs = _conv_ffn_layer(xs, m, norm_gain[layer, 1], ffn_w_up[layer], ffn_conv_w[layer],
                             ffn_conv_b[layer], ffn_w_down[layer])
    return xs[None]
```

```python
import functools

import jax
import jax.numpy as jnp
from jax import lax
from jax.experimental import pallas as pl
from jax.experimental.pallas import tpu as pltpu

F32 = jnp.float32
BF16 = jnp.bfloat16

D_MODEL = 2048
DEPTH = 4
N_A_LAYERS = 2
EPS = 1e-6
N_MOD = 6

CHUNK = 128
SGU_WIDTH = 2 * D_MODEL
SGU_GROUPS = 16
SGU_GROUP_DIM = SGU_WIDTH // SGU_GROUPS

D_FF = 11 * D_MODEL // 4
CONV_W = 3
CONV_HALO = 16

N_HEADS = 16
HEAD_DIM = 128
N_KV = 4
GROUP = 4
N_BRANCH = 3
N_KV_TENSORS = 6
CMP_LEN = 32
CMP_STRIDE = 16
SLC_LEN = 64
SLC_TOPK = 16
N_FORCED_LOCAL = 2
WINDOW = 512
Q_BLOCK = 128
NEG_INF = -1e30
FORCE_BONUS = 1e4
QK_SCALE = HEAD_DIM ** -0.5

KV_TILE = 512
WIN_KEYS = WINDOW + Q_BLOCK
GATE_ROWS = 16
AUG_BLOCKS = 16
AUG_OFFSET_BASE = 16
RANK_VARIANTS = 4

MIB = 1024 * 1024


def _params(sem, vmem_mib):
    return pltpu.CompilerParams(dimension_semantics=sem, vmem_limit_bytes=vmem_mib * MIB)


def _sigmoid(x):
    return 1.0 / (1.0 + jnp.exp(-x))


def _gelu_tanh(x):
    return 0.5 * x * (1.0 + jnp.tanh(0.7978845608028654 * (x + 0.044715 * (x * x * x))))


def _modnorm(x, gain, scale, shift):
    r = lax.rsqrt(jnp.mean(x * x, axis=-1, keepdims=True) + EPS)
    return (x * r) * gain * (1.0 + scale) + shift


def _head_rms(z, gain):
    return z * lax.rsqrt(jnp.mean(z * z, axis=-1, keepdims=True) + EPS) * gain


def _dot(a, b):
    return jnp.dot(a, b, preferred_element_type=F32)


def _adaln_kernel(c_ref, w_ref, b_ref, o_ref):
    c = c_ref[...]
    s = c * _sigmoid(c)
    o_ref[0] = jnp.sum(s * w_ref[0], axis=0, keepdims=True) + b_ref[0]


def _adaln(c_col, w, b, tn=1024):
    L, D, N = w.shape
    return pl.pallas_call(
        _adaln_kernel,
        grid=(L, N // tn),
        in_specs=[
            pl.BlockSpec((D, 1), lambda l, j: (0, 0)),
            pl.BlockSpec((1, D, tn), lambda l, j: (l, 0, j)),
            pl.BlockSpec((1, 1, tn), lambda l, j: (l, 0, j)),
        ],
        out_specs=pl.BlockSpec((1, 1, tn), lambda l, j: (l, 0, j)),
        out_shape=jax.ShapeDtypeStruct((L, 1, N), F32),
        compiler_params=_params(("arbitrary", "arbitrary"), 40),
        name="adaln",
    )(c_col, w, b.reshape(L, 1, N))


def _sgu_in_kernel(x_ref, g_ref, sc_ref, sh_ref, w_ref, uv_ref, ss_ref, h_scr, *, n_u_tiles):
    j = pl.program_id(1)

    @pl.when(j == 0)
    def _():
        h_scr[...] = _modnorm(x_ref[...], g_ref[...], sc_ref[...], sh_ref[...]).astype(BF16)
        ss_ref[...] = jnp.zeros_like(ss_ref)

    y = _gelu_tanh(_dot(h_scr[...], w_ref[...]))
    uv_ref[...] = y.astype(BF16)

    @pl.when(j >= n_u_tiles)
    def _():
        ss_ref[...] += jnp.sum(y * y, axis=-1, keepdims=True)


def _sgu_in(x, gain, scale, shift, w, tm=1024, tn=512):
    T, D = x.shape
    N = w.shape[1]
    vec = pl.BlockSpec((1, D), lambda i, j: (0, 0))
    return pl.pallas_call(
        functools.partial(_sgu_in_kernel, n_u_tiles=SGU_WIDTH // tn),
        grid=(T // tm, N // tn),
        in_specs=[pl.BlockSpec((tm, D), lambda i, j: (i, 0)), vec, vec, vec,
                  pl.BlockSpec((D, tn), lambda i, j: (0, j))],
        out_specs=[pl.BlockSpec((tm, tn), lambda i, j: (i, j)),
                   pl.BlockSpec((tm, 1), lambda i, j: (i, 0))],
        out_shape=[jax.ShapeDtypeStruct((T, N), BF16), jax.ShapeDtypeStruct((T, 1), F32)],
        scratch_shapes=[pltpu.VMEM((tm, D), BF16)],
        compiler_params=_params(("arbitrary", "arbitrary"), 48),
        name="sgu_in",
    )(x, gain, scale, shift, w)


def _sgu_gate_kernel(u_ref, v_ref, ss_ref, vg_ref, w_ref, b_ref, o_ref, *, n_chunks):
    row = lax.broadcasted_iota(jnp.int32, (CHUNK, CHUNK), 0)
    col = lax.broadcasted_iota(jnp.int32, (CHUNK, CHUNK), 1)
    wc = jnp.where(row >= col, w_ref[0], 0.0).astype(BF16)
    r = lax.rsqrt(ss_ref[...] * (1.0 / SGU_WIDTH) + EPS)
    v = (v_ref[...].astype(F32) * r * vg_ref[...]).astype(BF16)
    b = b_ref[0]
    for n in range(n_chunks):
        rows = slice(n * CHUNK, (n + 1) * CHUNK)
        sv = _dot(wc, v[rows]) + b
        o_ref[rows, :] = (u_ref[rows, :].astype(F32) * sv).astype(BF16)


def _sgu_gate(uv, ss, v_gain, w_s, b_s, tc=512):
    T = uv.shape[0]
    gd = SGU_GROUP_DIM
    return pl.pallas_call(
        functools.partial(_sgu_gate_kernel, n_chunks=tc // CHUNK),
        grid=(SGU_GROUPS, T // tc),
        in_specs=[
            pl.BlockSpec((tc, gd), lambda g, i: (i, g)),
            pl.BlockSpec((tc, gd), lambda g, i: (i, SGU_GROUPS + g)),
            pl.BlockSpec((tc, 1), lambda g, i: (i, 0)),
            pl.BlockSpec((1, gd), lambda g, i: (0, g)),
            pl.BlockSpec((1, CHUNK, CHUNK), lambda g, i: (g, 0, 0)),
            pl.BlockSpec((1, CHUNK, 1), lambda g, i: (g, 0, 0)),
        ],
        out_specs=pl.BlockSpec((tc, gd), lambda g, i: (i, g)),
        out_shape=jax.ShapeDtypeStruct((T, SGU_WIDTH), BF16),
        compiler_params=_params(("arbitrary", "arbitrary"), 32),
        name="sgu_gate",
    )(uv, uv, ss, v_gain.reshape(1, SGU_WIDTH), w_s, b_s.reshape(SGU_GROUPS, CHUNK, 1))


def _mm_res_kernel(a_ref, w_ref, x_ref, g_ref, o_ref):
    o_ref[...] = x_ref[...] + g_ref[...] * _dot(a_ref[...], w_ref[...])


def _mm_res(a, w, x, gate, tm=512, tn=512):
    T, K = a.shape
    N = w.shape[1]
    return pl.pallas_call(
        _mm_res_kernel,
        grid=(T // tm, N // tn),
        in_specs=[pl.BlockSpec((tm, K), lambda i, j: (i, 0)),
                  pl.BlockSpec((K, tn), lambda i, j: (0, j)),
                  pl.BlockSpec((tm, tn), lambda i, j: (i, j)),
                  pl.BlockSpec((1, tn), lambda i, j: (0, j))],
        out_specs=pl.BlockSpec((tm, tn), lambda i, j: (i, j)),
        out_shape=jax.ShapeDtypeStruct((T, N), F32),
        compiler_params=_params(("arbitrary", "arbitrary"), 48),
        name="mm_res",
    )(a, w, x, gate)


def _ffn_up_kernel(xh_ref, x_ref, g_ref, sc_ref, sh_ref, wa_ref, wv_ref, cwa_ref, cwv_ref,
                   cba_ref, cbv_ref, o_ref, h_scr, za_scr, zv_scr, *, tm):
    i = pl.program_id(0)
    j = pl.program_id(1)

    @pl.when(j == 0)
    def _():
        g, sc, sh = g_ref[...], sc_ref[...], sh_ref[...]
        h_scr[CONV_HALO:, :] = _modnorm(x_ref[...], g, sc, sh).astype(BF16)
        halo = _modnorm(xh_ref[...], g, sc, sh)
        h_scr[:CONV_HALO, :] = jnp.where(i > 0, halo, 0.0).astype(BF16)

    h = h_scr[...]
    za_scr[...] = _dot(h, wa_ref[...])
    zv_scr[...] = _dot(h, wv_ref[...])

    def conv(z_scr, cw_ref, cb_ref):
        acc = cb_ref[...] + cw_ref[0:1, :] * z_scr[pl.ds(CONV_HALO - 2, tm), :]
        acc = acc + cw_ref[1:2, :] * z_scr[pl.ds(CONV_HALO - 1, tm), :]
        return acc + cw_ref[2:3, :] * z_scr[pl.ds(CONV_HALO, tm), :]

    a = conv(za_scr, cwa_ref, cba_ref)
    v = conv(zv_scr, cwv_ref, cbv_ref)
    o_ref[...] = (a * _sigmoid(a) * v).astype(BF16)


def _ffn_up(x, gain, scale, shift, w_up, conv_w, conv_b, tm=1024, tn=512):
    T, D = x.shape
    nj = D_FF // tn
    hb = tm // CONV_HALO
    vec = pl.BlockSpec((1, D), lambda i, j: (0, 0))
    conv_b = conv_b.reshape(1, 2 * D_FF)
    return pl.pallas_call(
        functools.partial(_ffn_up_kernel, tm=tm),
        grid=(T // tm, nj),
        in_specs=[
            pl.BlockSpec((CONV_HALO, D), lambda i, j: (jnp.maximum(i * hb - 1, 0), 0)),
            pl.BlockSpec((tm, D), lambda i, j: (i, 0)),
            vec, vec, vec,
            pl.BlockSpec((D, tn), lambda i, j: (0, j)),
            pl.BlockSpec((D, tn), lambda i, j: (0, j + nj)),
            pl.BlockSpec((CONV_W, tn), lambda i, j: (0, j)),
            pl.BlockSpec((CONV_W, tn), lambda i, j: (0, j + nj)),
            pl.BlockSpec((1, tn), lambda i, j: (0, j)),
            pl.BlockSpec((1, tn), lambda i, j: (0, j + nj)),
        ],
        out_specs=pl.BlockSpec((tm, tn), lambda i, j: (i, j)),
        out_shape=jax.ShapeDtypeStruct((T, D_FF), BF16),
        scratch_shapes=[pltpu.VMEM((tm + CONV_HALO, D), BF16),
                        pltpu.VMEM((tm + CONV_HALO, tn), F32),
                        pltpu.VMEM((tm + CONV_HALO, tn), F32)],
        compiler_params=_params(("arbitrary", "arbitrary"), 52),
        name="ffn_up",
    )(x, x, gain, scale, shift, w_up, w_up, conv_w, conv_w, conv_b, conv_b)


def _kv_kernel(x_ref, g_ref, sc_ref, sh_ref, w_ref, kg_ref, o_ref, vt_ref, h_scr):
    j = pl.program_id(1)

    @pl.when(j == 0)
    def _():
        h_scr[...] = _modnorm(x_ref[...], g_ref[...], sc_ref[...], sh_ref[...]).astype(BF16)

    z = _dot(h_scr[...], w_ref[...])
    is_key_normed = (j == 2) | (j == 4)
    gain = kg_ref[0]
    for g in range(N_KV):
        zg = z[:, g * HEAD_DIM:(g + 1) * HEAD_DIM]
        o_ref[g] = jnp.where(is_key_normed, _head_rms(zg, gain), zg).astype(BF16)

    @pl.when((j == 3) | (j == 5))
    def _():
        for g in range(N_KV):
            vt_ref[0, g] = z[:, g * HEAD_DIM:(g + 1) * HEAD_DIM].T.astype(BF16)


def _kv_proj(x, gain, scale, shift, w, key_gain, tm=1024):
    T, D = x.shape
    tn = N_KV * HEAD_DIM
    vec = pl.BlockSpec((1, D), lambda i, j: (0, 0))
    return pl.pallas_call(
        _kv_kernel,
        grid=(T // tm, N_KV_TENSORS),
        in_specs=[pl.BlockSpec((tm, D), lambda i, j: (i, 0)), vec, vec, vec,
                  pl.BlockSpec((D, tn), lambda i, j: (0, j)),
                  pl.BlockSpec((1, 1, HEAD_DIM), lambda i, j: (j, 0, 0))],
        out_specs=[pl.BlockSpec((N_KV, tm, HEAD_DIM), lambda i, j: (j, i, 0)),
                   pl.BlockSpec((1, N_KV, HEAD_DIM, tm), lambda i, j: (j // 4, 0, 0, i))],
        out_shape=[jax.ShapeDtypeStruct((N_KV_TENSORS * N_KV, T, HEAD_DIM), BF16),
                   jax.ShapeDtypeStruct((2, N_KV, HEAD_DIM, T), BF16)],
        scratch_shapes=[pltpu.VMEM((tm, D), BF16)],
        compiler_params=_params(("arbitrary", "arbitrary"), 48),
        name="kv_proj",
    )(x, gain, scale, shift, w, key_gain)


def _cmp_kernel(seg_ref, pos_ref, w1_ref, w2_ref, kg_ref, o_ref, b_scr, *, n_seg, is_key):
    seg = seg_ref[0]
    w1 = w1_ref[0]
    half = CMP_STRIDE * HEAD_DIM
    first = _dot(seg, w1[:half].astype(BF16))
    second = _dot(seg, w1[half:].astype(BF16))
    pos_bias = jnp.sum(pos_ref[0] * w1, axis=0, keepdims=True)
    b_scr[pl.ds(0, n_seg), :] = second
    b_scr[pl.ds(n_seg, 8), :] = jnp.zeros((8, HEAD_DIM), F32)
    pre = first + b_scr[pl.ds(1, n_seg), :] + pos_bias
    out = _dot(_gelu_tanh(pre).astype(BF16), w2_ref[0].astype(BF16))
    if is_key:
        o_ref[0] = _head_rms(out, kg_ref[...]).astype(BF16)
    else:
        b_scr[pl.ds(0, n_seg), :] = out
        o_ref[0] = b_scr[pl.ds(0, n_seg), :].T.astype(BF16)


def _compress(kv, which, cmp_pos, cmp_w1, cmp_w2, k_gain_cmp):
    T = kv.shape[1]
    n_seg = T // CMP_STRIDE
    seg_w = CMP_STRIDE * HEAD_DIM
    segs = kv.reshape(kv.shape[0], n_seg, seg_w)
    flat = CMP_LEN * HEAD_DIM
    is_key = which == 0
    out_block = (1, n_seg, HEAD_DIM) if is_key else (1, HEAD_DIM, n_seg)
    return pl.pallas_call(
        functools.partial(_cmp_kernel, n_seg=n_seg, is_key=is_key),
        grid=(N_KV,),
        in_specs=[pl.BlockSpec((1, n_seg, seg_w), lambda g: (which * N_KV + g, 0, 0)),
                  pl.BlockSpec((1, flat, 1), lambda g: (which, 0, 0)),
                  pl.BlockSpec((1, flat, HEAD_DIM), lambda g: (which, 0, 0)),
                  pl.BlockSpec((1, HEAD_DIM, HEAD_DIM), lambda g: (which, 0, 0)),
                  pl.BlockSpec((1, HEAD_DIM), lambda g: (0, 0))],
        out_specs=pl.BlockSpec(out_block, lambda g: (g, 0, 0)),
        out_shape=jax.ShapeDtypeStruct((N_KV,) + out_block[1:], BF16),
        scratch_shapes=[pltpu.VMEM((n_seg + 8, HEAD_DIM), F32)],
        compiler_params=_params(("arbitrary",), 40),
        name="compress",
    )(segs, cmp_pos.reshape(2, flat, 1), cmp_w1, cmp_w2, k_gain_cmp.reshape(1, HEAD_DIM))


def _q_kernel(x_ref, g_ref, sc_ref, sh_ref, wq_ref, wg_ref, qg_ref, q_ref, gt_ref, h_scr):
    j = pl.program_id(1)

    @pl.when(j == 0)
    def _():
        h_scr[...] = _modnorm(x_ref[...], g_ref[...], sc_ref[...], sh_ref[...]).astype(BF16)
        gates = _sigmoid(_dot(h_scr[...], wg_ref[...]))
        for g in range(N_KV):
            gt_ref[g] = gates[:, g * 128:(g + 1) * 128].T[:GATE_ROWS]

    z = _dot(h_scr[...], wq_ref[...])
    gain = qg_ref[...] * QK_SCALE
    for r in range(GROUP):
        cols = slice(r * HEAD_DIM, (r + 1) * HEAD_DIM)
        q_ref[r] = _head_rms(z[:, cols], gain).T.astype(BF16)


def _q_proj(x, gain, scale, shift, w_q, w_gate, q_gain, tm=1024):
    T, D = x.shape
    tn = GROUP * HEAD_DIM
    vec = pl.BlockSpec((1, D), lambda i, j: (0, 0))
    ng = w_gate.shape[1]
    return pl.pallas_call(
        _q_kernel,
        grid=(T // tm, N_KV),
        in_specs=[pl.BlockSpec((tm, D), lambda i, j: (i, 0)), vec, vec, vec,
                  pl.BlockSpec((D, tn), lambda i, j: (0, j)),
                  pl.BlockSpec((D, ng), lambda i, j: (0, 0)),
                  pl.BlockSpec((1, HEAD_DIM), lambda i, j: (0, 0))],
        out_specs=[pl.BlockSpec((GROUP, HEAD_DIM, tm), lambda i, j: (j, 0, i)),
                   pl.BlockSpec((N_KV, GATE_ROWS, tm), lambda i, j: (0, 0, i))],
        out_shape=[jax.ShapeDtypeStruct((N_HEADS, HEAD_DIM, T), BF16),
                   jax.ShapeDtypeStruct((N_KV, GATE_ROWS, T), F32)],
        scratch_shapes=[pltpu.VMEM((tm, D), BF16)],
        compiler_params=_params(("arbitrary", "arbitrary"), 48),
        name="q_proj",
    )(x, gain, scale, shift, w_q, w_gate, q_gain.reshape(1, HEAD_DIM))


def _split3_bf16(x):
    hi = x.astype(BF16)
    r1 = x - hi.astype(F32)
    mid = r1.astype(BF16)
    lo = (r1 - mid.astype(F32)).astype(BF16)
    return hi, mid, lo


def _softmax_over_keys(s, valid):
    s = jnp.where(valid, s, NEG_INF)
    m = jnp.max(s, axis=0, keepdims=True)
    e = jnp.where(valid, jnp.exp(s - m), 0.0)
    return e / jnp.maximum(jnp.sum(e, axis=0, keepdims=True), 1e-30)


def _attn_t_kernel(q_ref, gt_ref, sl_ref, qx_ref, kc_ref, vc_ref, ks_ref, vs_ref, kw_ref, vw_ref,
                   o_ref, imp_scr, keep_scr, nk_scr, qa_scr, m_scr, l_scr, acc_scr, kept_smem, *, n_cmp, n_slc):
    qb = pl.program_id(1)
    t0 = qb * Q_BLOCK
    lanes = GROUP * Q_BLOCK
    slope = sl_ref[0]
    q_t = jnp.concatenate([q_ref[r] for r in range(GROUP)], axis=1)
    t_rel = lax.broadcasted_iota(jnp.int32, (1, lanes), 1) % Q_BLOCK

    s = _dot(kc_ref[0], q_t)
    ci = lax.broadcasted_iota(jnp.int32, (n_cmp, lanes), 0)
    d = ((t0 + t_rel) - (ci * CMP_STRIDE + (CMP_LEN - 1))).astype(F32)
    p = _softmax_over_keys(s - slope * d, d >= 0)
    o_cmp = _dot(vc_ref[0], p.astype(BF16))

    p_group = p[:, 0:Q_BLOCK]
    for r in range(1, GROUP):
        p_group = p_group + p[:, r * Q_BLOCK:(r + 1) * Q_BLOCK]
    oj = lax.broadcasted_iota(jnp.int32, (n_slc, n_cmp), 0) * (SLC_LEN // CMP_STRIDE)
    oi = lax.broadcasted_iota(jnp.int32, (n_slc, n_cmp), 1)
    overlap = ((oi > oj - CMP_LEN // CMP_STRIDE) & (oi < oj + SLC_LEN // CMP_STRIDE)).astype(BF16)
    imp = sum(_dot(overlap, part) for part in _split3_bf16(p_group))
    blk = lax.broadcasted_iota(jnp.int32, (n_slc, Q_BLOCK), 0)
    cur = (t0 + lax.broadcasted_iota(jnp.int32, (n_slc, Q_BLOCK), 1)) // SLC_LEN
    forced = (blk == 0) | ((blk <= cur) & (blk > cur - N_FORCED_LOCAL))
    imp = jnp.where(forced, imp + FORCE_BONUS, imp)
    imp = jnp.where(blk <= cur, imp, NEG_INF)

    imp_scr[...] = imp
    keep_scr[...] = jnp.zeros(keep_scr.shape, F32)
    sub = lax.broadcasted_iota(jnp.int32, (8, Q_BLOCK), 0)
    last_blk = (t0 + Q_BLOCK - 1) // SLC_LEN
    rank_step = max(n_slc // RANK_VARIANTS, 8)

    def count_leading(n_lead):
        n_grp = n_lead // 8
        imp_rows = [imp_scr[8 * v:8 * v + 8, :] for v in range(n_grp)]
        rank = [jnp.zeros((8, Q_BLOCK), jnp.int32) for _ in range(n_grp)]
        for jp in range(n_lead):
            other = imp_scr[jp:jp + 1, :]
            for v in range(n_grp):
                if 8 * v > jp:
                    beats = other >= imp_rows[v]
                elif 8 * v + 7 < jp:
                    beats = other > imp_rows[v]
                else:
                    beats = (other > imp_rows[v]) | ((other == imp_rows[v]) & (sub > jp - 8 * v))
                rank[v] = rank[v] + jnp.where(beats, 1, 0)
        rank = jnp.concatenate(rank, axis=0)
        lead_blk = lax.broadcasted_iota(jnp.int32, (n_lead, Q_BLOCK), 0)
        lead_cur = (t0 + lax.broadcasted_iota(jnp.int32, (n_lead, Q_BLOCK), 1)) // SLC_LEN
        keep = (rank < SLC_TOPK) & (lead_blk <= lead_cur)
        keep_scr[0:n_lead, :] = jnp.where(keep, 1.0, 0.0)

    for n_lead in range(rank_step, n_slc + 1, rank_step):
        @pl.when(last_blk // rank_step == n_lead // rank_step - 1)
        def _(n_lead=n_lead):
            count_leading(n_lead)

    not_kept = jnp.where(keep_scr[...] > 0.0, 0.0, NEG_INF).astype(BF16)
    for r in range(GROUP):
        nk_scr[:, r * Q_BLOCK:(r + 1) * Q_BLOCK] = not_kept

    qa_scr[0:HEAD_DIM, :] = q_t
    qa_scr[HEAD_DIM + AUG_BLOCKS:, :] = qx_ref[0]

    m_scr[...] = jnp.full(m_scr.shape, NEG_INF, F32)
    l_scr[...] = jnp.zeros(l_scr.shape, F32)
    acc_scr[...] = jnp.zeros(acc_scr.shape, F32)
    krow = lax.broadcasted_iota(jnp.int32, (KV_TILE, lanes), 0)
    n_full = t0 // KV_TILE

    def scores(kt):
        k0 = pl.multiple_of(kt * KV_TILE, KV_TILE)
        first_blk = pl.multiple_of((kt * (KV_TILE // SLC_LEN) // AUG_BLOCKS) * AUG_BLOCKS, AUG_BLOCKS)
        qa_scr[HEAD_DIM:HEAD_DIM + AUG_BLOCKS, :] = nk_scr[pl.ds(first_blk, AUG_BLOCKS), :]
        return _dot(ks_ref[0, pl.ds(k0, KV_TILE), :], qa_scr[...])

    def accumulate(s, kt, causal):
        k0 = pl.multiple_of(kt * KV_TILE, KV_TILE)
        off = t_rel + (t0 - k0)
        if causal:
            s = jnp.where(krow <= off, s, NEG_INF)
        col_term = slope * off.astype(F32)
        m_old = m_scr[...]
        m_new = jnp.maximum(m_old, jnp.max(s, axis=0, keepdims=True) - col_term)
        e = jnp.exp(s - (m_new + col_term))
        alpha = jnp.exp(m_old - m_new)
        l_scr[...] = alpha * l_scr[...] + jnp.sum(e, axis=0, keepdims=True)
        acc_scr[...] = alpha * acc_scr[...] + _dot(vs_ref[0, :, pl.ds(k0, KV_TILE)], e.astype(BF16))
        m_scr[...] = m_new

    blocks_per_tile = KV_TILE // SLC_LEN
    for tile in range(n_slc // blocks_per_tile):
        kept_smem[tile] = jnp.sum(keep_scr[tile * blocks_per_tile:(tile + 1) * blocks_per_tile, :])

    def full_tile(kt, carry):
        @pl.when(kept_smem[kt] > 0.0)
        def _():
            accumulate(scores(kt), kt, causal=False)

        return carry

    lax.fori_loop(0, n_full, full_tile, 0)
    accumulate(scores(n_full), n_full, causal=True)
    o_slc = acc_scr[...] / l_scr[...]

    w0 = pl.multiple_of(jnp.maximum(t0 - WINDOW, 0), Q_BLOCK)
    s = _dot(kw_ref[0, pl.ds(w0, WIN_KEYS), :], q_t)
    wrow = lax.broadcasted_iota(jnp.int32, (WIN_KEYS, lanes), 0)
    d = ((t0 - w0) + t_rel - wrow).astype(F32)
    p = _softmax_over_keys(s - slope * d, (d >= 0) & (d < WINDOW))
    o_win = _dot(vw_ref[0, :, pl.ds(w0, WIN_KEYS)], p.astype(BF16))

    gates = gt_ref[0]

    def gate_lanes(branch):
        return jnp.concatenate([gates[branch * GROUP + r:branch * GROUP + r + 1, :] for r in range(GROUP)], axis=1)

    o_t = gate_lanes(0) * o_cmp + gate_lanes(1) * o_slc + gate_lanes(2) * o_win
    for r in range(GROUP):
        o_ref[:, r * HEAD_DIM:(r + 1) * HEAD_DIM] = o_t[:, r * Q_BLOCK:(r + 1) * Q_BLOCK].T.astype(BF16)


def _attention_t(q_t, gates_t, slopes, q_extra, k_cmp, v_cmp_t, k_slc_aug, kv, v_t):
    T = q_t.shape[2]
    n_cmp = T // CMP_STRIDE
    n_slc = T // SLC_LEN
    lanes = GROUP * Q_BLOCK
    hw = GROUP * HEAD_DIM
    n_extra = q_extra.shape[1]

    def per_group(shape):
        return pl.BlockSpec((1,) + shape, lambda g, qb: (g, 0, 0))

    return pl.pallas_call(
        functools.partial(_attn_t_kernel, n_cmp=n_cmp, n_slc=n_slc),
        grid=(N_KV, T // Q_BLOCK),
        in_specs=[pl.BlockSpec((GROUP, HEAD_DIM, Q_BLOCK), lambda g, qb: (g, 0, qb)),
                  pl.BlockSpec((1, GATE_ROWS, Q_BLOCK), lambda g, qb: (g, 0, qb)),
                  per_group((1, lanes)),
                  per_group((n_extra, lanes)),
                  per_group((n_cmp, HEAD_DIM)),
                  per_group((HEAD_DIM, n_cmp)),
                  per_group((T, 2 * HEAD_DIM)),
                  per_group((HEAD_DIM, T)),
                  pl.BlockSpec((1, T, HEAD_DIM), lambda g, qb: (4 * N_KV + g, 0, 0)),
                  pl.BlockSpec((1, HEAD_DIM, T), lambda g, qb: (N_KV + g, 0, 0))],
        out_specs=pl.BlockSpec((Q_BLOCK, hw), lambda g, qb: (qb, g)),
        out_shape=jax.ShapeDtypeStruct((T, N_HEADS * HEAD_DIM), BF16),
        scratch_shapes=[pltpu.VMEM((n_slc, Q_BLOCK), F32),
                        pltpu.VMEM((n_slc, Q_BLOCK), F32),
                        pltpu.VMEM((n_slc, lanes), BF16),
                        pltpu.VMEM((2 * HEAD_DIM, lanes), BF16),
                        pltpu.VMEM((1, lanes), F32),
                        pltpu.VMEM((1, lanes), F32),
                        pltpu.VMEM((HEAD_DIM, lanes), F32),
                        pltpu.SMEM((T // KV_TILE,), F32)],
        compiler_params=_params(("arbitrary", "arbitrary"), 48),
        name="nsa_attention",
    )(q_t, gates_t, slopes, q_extra, k_cmp, v_cmp_t, k_slc_aug, v_t, kv, v_t)


def _gate_weight(w_in):
    D = w_in.shape[0]
    wg = w_in[:, N_HEADS * HEAD_DIM:].reshape(D, N_BRANCH, N_KV, GROUP).transpose(0, 2, 1, 3)
    wg = wg.reshape(D, N_KV, N_BRANCH * GROUP)
    wg = jnp.pad(wg, ((0, 0), (0, 0), (0, 128 - N_BRANCH * GROUP)))
    return wg.reshape(D, N_KV * 128).astype(BF16)


def _alibi_and_mask_columns(T):
    h = jnp.arange(1, N_HEADS + 1, dtype=F32)
    slopes = (2.0 ** (-8.0 * h / N_HEADS)).reshape(N_KV, GROUP)
    slopes = jnp.repeat(slopes, Q_BLOCK, axis=1)[:, None, :]
    hi, mid, lo = _split3_bf16(slopes)
    base = float(AUG_OFFSET_BASE)
    rows = [(hi.astype(F32) * base).astype(BF16), (mid.astype(F32) * base).astype(BF16),
            (lo.astype(F32) * base).astype(BF16), hi, mid, lo]
    n_pad = HEAD_DIM - AUG_BLOCKS - len(rows)
    q_extra = jnp.concatenate(rows + [jnp.zeros((N_KV, n_pad, GROUP * Q_BLOCK), BF16)], axis=1)

    kpos = jnp.arange(T, dtype=jnp.int32)
    one_hot = ((kpos // SLC_LEN) % AUG_BLOCKS)[:, None] == jnp.arange(AUG_BLOCKS, dtype=jnp.int32)[None, :]
    off = kpos % KV_TILE
    a = (off // AUG_OFFSET_BASE)[:, None]
    b = (off % AUG_OFFSET_BASE)[:, None]
    k_extra = jnp.concatenate([one_hot.astype(jnp.int32), a, a, a, b, b, b,
                               jnp.zeros((T, n_pad), jnp.int32)], axis=1).astype(BF16)
    return slopes, q_extra, k_extra


def _conv_ffn_layer(x, m, norm_gain, w_up, conv_w, conv_b, w_down):
    D = D_MODEL
    sh2, sc2, g2 = m[:, 3 * D:4 * D], m[:, 4 * D:5 * D], m[:, 5 * D:6 * D]
    act = _ffn_up(x, norm_gain.reshape(1, D), sc2, sh2, w_up.astype(BF16), conv_w, conv_b)
    return _mm_res(act, w_down.astype(BF16), x, g2)


def kernel(x, c, mod_w, mod_b, norm_gain, ffn_w_up, ffn_conv_w, ffn_conv_b, ffn_w_down, sgu_w_in, sgu_v_gain, sgu_w_s, sgu_b_s, sgu_w_out, kv_gain, kv_mod_w, kv_mod_b, w_kv, cmp_pos, cmp_w1, cmp_w2, k_gain, nsa_w_in, nsa_q_gain, nsa_w_o):
    B, T, D = x.shape
    assert B == 1 and D == D_MODEL and T % 1024 == 0
    xs = x[0]
    c_col = c.reshape(D, 1)
    mods = _adaln(c_col, mod_w, mod_b)
    kv_mod = _adaln(c_col, kv_mod_w[None], kv_mod_b[None])[0]

    slopes, q_extra, k_extra = _alibi_and_mask_columns(T)
    kv = v_t = k_cmp = v_cmp_t = k_slc_aug = None
    for layer in range(DEPTH):
        m = mods[layer]
        sh1, sc1, g1 = m[:, 0:D], m[:, D:2 * D], m[:, 2 * D:3 * D]
        gain1 = norm_gain[layer, 0].reshape(1, D)
        if layer == N_A_LAYERS:
            key_gain = jnp.ones((N_KV_TENSORS, 1, HEAD_DIM), F32).at[2, 0].set(k_gain[1]).at[4, 0].set(k_gain[2])
            kv, v_t = _kv_proj(xs, kv_gain.reshape(1, D), kv_mod[:, D:], kv_mod[:, :D], w_kv.astype(BF16), key_gain)
            v_t = v_t.reshape(2 * N_KV, HEAD_DIM, T)
            k_cmp = _compress(kv, 0, cmp_pos, cmp_w1, cmp_w2, k_gain[0])
            v_cmp_t = _compress(kv, 1, cmp_pos, cmp_w1, cmp_w2, k_gain[0])
            k_slc_aug = jnp.concatenate(
                [kv[2 * N_KV:3 * N_KV], jnp.broadcast_to(k_extra, (N_KV, T, HEAD_DIM))], axis=-1)
        if layer < N_A_LAYERS:
            a = layer
            uv, ss = _sgu_in(xs, gain1, sc1, sh1, sgu_w_in[a].astype(BF16))
            y = _sgu_gate(uv, ss, sgu_v_gain[a], sgu_w_s[a], sgu_b_s[a])
            xs = _mm_res(y, sgu_w_out[a].astype(BF16), xs, g1)
        else:
            bl = layer - N_A_LAYERS
            w_in = nsa_w_in[bl]
            q_t, gates_t = _q_proj(xs, gain1, sc1, sh1, w_in[:, :N_HEADS * HEAD_DIM].astype(BF16),
                                   _gate_weight(w_in), nsa_q_gain[bl])
            o = _attention_t(q_t, gates_t, slopes, q_extra, k_cmp, v_cmp_t, k_slc_aug, kv, v_t)
            xs = _mm_res(o, nsa_w_o[bl].astype(BF16), xs, g1)
        xs = _conv_ffn_layer(xs, m, norm_gain[layer, 1], ffn_w_up[layer], ffn_conv_w[layer],
                             ffn_conv_b[layer], ffn_w_down[layer])
    return xs[None]
```

```python
import functools

import jax
import jax.numpy as jnp
from jax import lax
from jax.experimental import pallas as pl
from jax.experimental.pallas import tpu as pltpu

F32 = jnp.float32
BF16 = jnp.bfloat16

D_MODEL = 2048
DEPTH = 4
N_A_LAYERS = 2
EPS = 1e-6
N_MOD = 6

CHUNK = 128
SGU_WIDTH = 2 * D_MODEL
SGU_GROUPS = 16
SGU_GROUP_DIM = SGU_WIDTH // SGU_GROUPS

D_FF = 11 * D_MODEL // 4
CONV_W = 3
CONV_HALO = 16

N_HEADS = 16
HEAD_DIM = 128
N_KV = 4
GROUP = 4
N_BRANCH = 3
N_KV_TENSORS = 6
CMP_LEN = 32
CMP_STRIDE = 16
SLC_LEN = 64
SLC_TOPK = 16
N_FORCED_LOCAL = 2
WINDOW = 512
Q_BLOCK = 128
NEG_INF = -1e30
FORCE_BONUS = 1e4
QK_SCALE = HEAD_DIM ** -0.5

KV_TILE = 512
WIN_KEYS = WINDOW + Q_BLOCK
GATE_ROWS = 16
AUG_BLOCKS = 16
AUG_OFFSET_BASE = 16
RANK_VARIANTS = 4

MIB = 1024 * 1024


def _params(sem, vmem_mib):
    return pltpu.CompilerParams(dimension_semantics=sem, vmem_limit_bytes=vmem_mib * MIB)


def _sigmoid(x):
    return 1.0 / (1.0 + jnp.exp(-x))


def _gelu_tanh(x):
    return 0.5 * x * (1.0 + jnp.tanh(0.7978845608028654 * (x + 0.044715 * (x * x * x))))


def _modnorm(x, gain, scale, shift):
    r = lax.rsqrt(jnp.mean(x * x, axis=-1, keepdims=True) + EPS)
    return (x * r) * gain * (1.0 + scale) + shift


def _head_rms(z, gain):
    return z * lax.rsqrt(jnp.mean(z * z, axis=-1, keepdims=True) + EPS) * gain


def _dot(a, b):
    return jnp.dot(a, b, preferred_element_type=F32)


def _adaln_kernel(c_ref, w_ref, b_ref, o_ref):
    c = c_ref[...]
    s = c * _sigmoid(c)
    o_ref[0] = jnp.sum(s * w_ref[0], axis=0, keepdims=True) + b_ref[0]


def _adaln(c_col, w, b, tn=1024):
    L, D, N = w.shape
    return pl.pallas_call(
        _adaln_kernel,
        grid=(L, N // tn),
        in_specs=[
            pl.BlockSpec((D, 1), lambda l, j: (0, 0)),
            pl.BlockSpec((1, D, tn), lambda l, j: (l, 0, j)),
            pl.BlockSpec((1, 1, tn), lambda l, j: (l, 0, j)),
        ],
        out_specs=pl.BlockSpec((1, 1, tn), lambda l, j: (l, 0, j)),
        out_shape=jax.ShapeDtypeStruct((L, 1, N), F32),
        compiler_params=_params(("arbitrary", "arbitrary"), 40),
        name="adaln",
    )(c_col, w, b.reshape(L, 1, N))


def _sgu_in_kernel(x_ref, g_ref, sc_ref, sh_ref, w_ref, uv_ref, ss_ref, h_scr, *, n_u_tiles):
    j = pl.program_id(1)

    @pl.when(j == 0)
    def _():
        h_scr[...] = _modnorm(x_ref[...], g_ref[...], sc_ref[...], sh_ref[...]).astype(BF16)
        ss_ref[...] = jnp.zeros_like(ss_ref)

    y = _gelu_tanh(_dot(h_scr[...], w_ref[...]))
    uv_ref[...] = y.astype(BF16)

    @pl.when(j >= n_u_tiles)
    def _():
        ss_ref[...] += jnp.sum(y * y, axis=-1, keepdims=True)


def _sgu_in(x, gain, scale, shift, w, tm=1024, tn=512):
    T, D = x.shape
    N = w.shape[1]
    vec = pl.BlockSpec((1, D), lambda i, j: (0, 0))
    return pl.pallas_call(
        functools.partial(_sgu_in_kernel, n_u_tiles=SGU_WIDTH // tn),
        grid=(T // tm, N // tn),
        in_specs=[pl.BlockSpec((tm, D), lambda i, j: (i, 0)), vec, vec, vec,
                  pl.BlockSpec((D, tn), lambda i, j: (0, j))],
        out_specs=[pl.BlockSpec((tm, tn), lambda i, j: (i, j)),
                   pl.BlockSpec((tm, 1), lambda i, j: (i, 0))],
        out_shape=[jax.ShapeDtypeStruct((T, N), BF16), jax.ShapeDtypeStruct((T, 1), F32)],
        scratch_shapes=[pltpu.VMEM((tm, D), BF16)],
        compiler_params=_params(("arbitrary", "arbitrary"), 48),
        name="sgu_in",
    )(x, gain, scale, shift, w)


def _sgu_gate_kernel(u_ref, v_ref, ss_ref, vg_ref, w_ref, b_ref, o_ref, *, n_chunks):
    row = lax.broadcasted_iota(jnp.int32, (CHUNK, CHUNK), 0)
    col = lax.broadcasted_iota(jnp.int32, (CHUNK, CHUNK), 1)
    wc = jnp.where(row >= col, w_ref[0], 0.0).astype(BF16)
    r = lax.rsqrt(ss_ref[...] * (1.0 / SGU_WIDTH) + EPS)
    v = (v_ref[...].astype(F32) * r * vg_ref[...]).astype(BF16)
    b = b_ref[0]
    for n in range(n_chunks):
        rows = slice(n * CHUNK, (n + 1) * CHUNK)
        sv = _dot(wc, v[rows]) + b
        o_ref[rows, :] = (u_ref[rows, :].astype(F32) * sv).astype(BF16)


def _sgu_gate(uv, ss, v_gain, w_s, b_s, tc=512):
    T = uv.shape[0]
    gd = SGU_GROUP_DIM
    return pl.pallas_call(
        functools.partial(_sgu_gate_kernel, n_chunks=tc // CHUNK),
        grid=(SGU_GROUPS, T // tc),
        in_specs=[
            pl.BlockSpec((tc, gd), lambda g, i: (i, g)),
            pl.BlockSpec((tc, gd), lambda g, i: (i, SGU_GROUPS + g)),
            pl.BlockSpec((tc, 1), lambda g, i: (i, 0)),
            pl.BlockSpec((1, gd), lambda g, i: (0, g)),
            pl.BlockSpec((1, CHUNK, CHUNK), lambda g, i: (g, 0, 0)),
            pl.BlockSpec((1, CHUNK, 1), lambda g, i: (g, 0, 0)),
        ],
        out_specs=pl.BlockSpec((tc, gd), lambda g, i: (i, g)),
        out_shape=jax.ShapeDtypeStruct((T, SGU_WIDTH), BF16),
        compiler_params=_params(("arbitrary", "arbitrary"), 32),
        name="sgu_gate",
    )(uv, uv, ss, v_gain.reshape(1, SGU_WIDTH), w_s, b_s.reshape(SGU_GROUPS, CHUNK, 1))


def _mm_res_kernel(a_ref, w_ref, x_ref, g_ref, o_ref):
    o_ref[...] = x_ref[...] + g_ref[...] * _dot(a_ref[...], w_ref[...])


def _mm_res(a, w, x, gate, tm=1024, tn=512):
    T, K = a.shape
    N = w.shape[1]
    return pl.pallas_call(
        _mm_res_kernel,
        grid=(T // tm, N // tn),
        in_specs=[pl.BlockSpec((tm, K), lambda i, j: (i, 0)),
                  pl.BlockSpec((K, tn), lambda i, j: (0, j)),
                  pl.BlockSpec((tm, tn), lambda i, j: (i, j)),
                  pl.BlockSpec((1, tn), lambda i, j: (0, j))],
        out_specs=pl.BlockSpec((tm, tn), lambda i, j: (i, j)),
        out_shape=jax.ShapeDtypeStruct((T, N), F32),
        compiler_params=_params(("arbitrary", "arbitrary"), 48),
        name="mm_res",
    )(a, w, x, gate)


def _ffn_up_kernel(xh_ref, x_ref, g_ref, sc_ref, sh_ref, wa_ref, wv_ref, cwa_ref, cwv_ref,
                   cba_ref, cbv_ref, o_ref, h_scr, za_scr, zv_scr, *, tm):
    i = pl.program_id(0)
    j = pl.program_id(1)

    @pl.when(j == 0)
    def _():
        g, sc, sh = g_ref[...], sc_ref[...], sh_ref[...]
        h_scr[CONV_HALO:, :] = _modnorm(x_ref[...], g, sc, sh).astype(BF16)
        halo = _modnorm(xh_ref[...], g, sc, sh)
        h_scr[:CONV_HALO, :] = jnp.where(i > 0, halo, 0.0).astype(BF16)

    h = h_scr[...]
    za_scr[...] = _dot(h, wa_ref[...])
    zv_scr[...] = _dot(h, wv_ref[...])

    def conv(z_scr, cw_ref, cb_ref):
        acc = cb_ref[...] + cw_ref[0:1, :] * z_scr[pl.ds(CONV_HALO - 2, tm), :]
        acc = acc + cw_ref[1:2, :] * z_scr[pl.ds(CONV_HALO - 1, tm), :]
        return acc + cw_ref[2:3, :] * z_scr[pl.ds(CONV_HALO, tm), :]

    a = conv(za_scr, cwa_ref, cba_ref)
    v = conv(zv_scr, cwv_ref, cbv_ref)
    o_ref[...] = (a * _sigmoid(a) * v).astype(BF16)


def _ffn_up(x, gain, scale, shift, w_up, conv_w, conv_b, tm=1024, tn=512):
    T, D = x.shape
    nj = D_FF // tn
    hb = tm // CONV_HALO
    vec = pl.BlockSpec((1, D), lambda i, j: (0, 0))
    conv_b = conv_b.reshape(1, 2 * D_FF)
    return pl.pallas_call(
        functools.partial(_ffn_up_kernel, tm=tm),
        grid=(T // tm, nj),
        in_specs=[
            pl.BlockSpec((CONV_HALO, D), lambda i, j: (jnp.maximum(i * hb - 1, 0), 0)),
            pl.BlockSpec((tm, D), lambda i, j: (i, 0)),
            vec, vec, vec,
            pl.BlockSpec((D, tn), lambda i, j: (0, j)),
            pl.BlockSpec((D, tn), lambda i, j: (0, j + nj)),
            pl.BlockSpec((CONV_W, tn), lambda i, j: (0, j)),
            pl.BlockSpec((CONV_W, tn), lambda i, j: (0, j + nj)),
            pl.BlockSpec((1, tn), lambda i, j: (0, j)),
            pl.BlockSpec((1, tn), lambda i, j: (0, j + nj)),
        ],
        out_specs=pl.BlockSpec((tm, tn), lambda i, j: (i, j)),
        out_shape=jax.ShapeDtypeStruct((T, D_FF), BF16),
        scratch_shapes=[pltpu.VMEM((tm + CONV_HALO, D), BF16),
                        pltpu.VMEM((tm + CONV_HALO, tn), F32),
                        pltpu.VMEM((tm + CONV_HALO, tn), F32)],
        compiler_params=_params(("arbitrary", "arbitrary"), 52),
        name="ffn_up",
    )(x, x, gain, scale, shift, w_up, w_up, conv_w, conv_w, conv_b, conv_b)


def _kv_kernel(x_ref, g_ref, sc_ref, sh_ref, w_ref, kg_ref, o_ref, vt_ref, h_scr):
    j = pl.program_id(1)

    @pl.when(j == 0)
    def _():
        h_scr[...] = _modnorm(x_ref[...], g_ref[...], sc_ref[...], sh_ref[...]).astype(BF16)

    z = _dot(h_scr[...], w_ref[...])
    is_key_normed = (j == 2) | (j == 4)
    gain = kg_ref[0]
    for g in range(N_KV):
        zg = z[:, g * HEAD_DIM:(g + 1) * HEAD_DIM]
        o_ref[g] = jnp.where(is_key_normed, _head_rms(zg, gain), zg).astype(BF16)

    @pl.when((j == 3) | (j == 5))
    def _():
        for g in range(N_KV):
            vt_ref[0, g] = z[:, g * HEAD_DIM:(g + 1) * HEAD_DIM].T.astype(BF16)


def _kv_proj(x, gain, scale, shift, w, key_gain, tm=1024):
    T, D = x.shape
    tn = N_KV * HEAD_DIM
    vec = pl.BlockSpec((1, D), lambda i, j: (0, 0))
    return pl.pallas_call(
        _kv_kernel,
        grid=(T // tm, N_KV_TENSORS),
        in_specs=[pl.BlockSpec((tm, D), lambda i, j: (i, 0)), vec, vec, vec,
                  pl.BlockSpec((D, tn), lambda i, j: (0, j)),
                  pl.BlockSpec((1, 1, HEAD_DIM), lambda i, j: (j, 0, 0))],
        out_specs=[pl.BlockSpec((N_KV, tm, HEAD_DIM), lambda i, j: (j, i, 0)),
                   pl.BlockSpec((1, N_KV, HEAD_DIM, tm), lambda i, j: (j // 4, 0, 0, i))],
        out_shape=[jax.ShapeDtypeStruct((N_KV_TENSORS * N_KV, T, HEAD_DIM), BF16),
                   jax.ShapeDtypeStruct((2, N_KV, HEAD_DIM, T), BF16)],
        scratch_shapes=[pltpu.VMEM((tm, D), BF16)],
        compiler_params=_params(("arbitrary", "arbitrary"), 48),
        name="kv_proj",
    )(x, gain, scale, shift, w, key_gain)


def _cmp_kernel(seg_ref, pos_ref, w1_ref, w2_ref, kg_ref, o_ref, b_scr, *, n_seg, is_key):
    seg = seg_ref[0]
    w1 = w1_ref[0]
    half = CMP_STRIDE * HEAD_DIM
    first = _dot(seg, w1[:half].astype(BF16))
    second = _dot(seg, w1[half:].astype(BF16))
    pos_bias = jnp.sum(pos_ref[0] * w1, axis=0, keepdims=True)
    b_scr[pl.ds(0, n_seg), :] = second
    b_scr[pl.ds(n_seg, 8), :] = jnp.zeros((8, HEAD_DIM), F32)
    pre = first + b_scr[pl.ds(1, n_seg), :] + pos_bias
    out = _dot(_gelu_tanh(pre).astype(BF16), w2_ref[0].astype(BF16))
    if is_key:
        o_ref[0] = _head_rms(out, kg_ref[...]).astype(BF16)
    else:
        b_scr[pl.ds(0, n_seg), :] = out
        o_ref[0] = b_scr[pl.ds(0, n_seg), :].T.astype(BF16)


def _compress(kv, which, cmp_pos, cmp_w1, cmp_w2, k_gain_cmp):
    T = kv.shape[1]
    n_seg = T // CMP_STRIDE
    seg_w = CMP_STRIDE * HEAD_DIM
    segs = kv.reshape(kv.shape[0], n_seg, seg_w)
    flat = CMP_LEN * HEAD_DIM
    is_key = which == 0
    out_block = (1, n_seg, HEAD_DIM) if is_key else (1, HEAD_DIM, n_seg)
    return pl.pallas_call(
        functools.partial(_cmp_kernel, n_seg=n_seg, is_key=is_key),
        grid=(N_KV,),
        in_specs=[pl.BlockSpec((1, n_seg, seg_w), lambda g: (which * N_KV + g, 0, 0)),
                  pl.BlockSpec((1, flat, 1), lambda g: (which, 0, 0)),
                  pl.BlockSpec((1, flat, HEAD_DIM), lambda g: (which, 0, 0)),
                  pl.BlockSpec((1, HEAD_DIM, HEAD_DIM), lambda g: (which, 0, 0)),
                  pl.BlockSpec((1, HEAD_DIM), lambda g: (0, 0))],
        out_specs=pl.BlockSpec(out_block, lambda g: (g, 0, 0)),
        out_shape=jax.ShapeDtypeStruct((N_KV,) + out_block[1:], BF16),
        scratch_shapes=[pltpu.VMEM((n_seg + 8, HEAD_DIM), F32)],
        compiler_params=_params(("arbitrary",), 40),
        name="compress",
    )(segs, cmp_pos.reshape(2, flat, 1), cmp_w1, cmp_w2, k_gain_cmp.reshape(1, HEAD_DIM))


def _q_kernel(x_ref, g_ref, sc_ref, sh_ref, wq_ref, wg_ref, qg_ref, q_ref, gt_ref, h_scr):
    j = pl.program_id(1)

    @pl.when(j == 0)
    def _():
        h_scr[...] = _modnorm(x_ref[...], g_ref[...], sc_ref[...], sh_ref[...]).astype(BF16)
        gates = _sigmoid(_dot(h_scr[...], wg_ref[...]))
        for g in range(N_KV):
            gt_ref[g] = gates[:, g * 128:(g + 1) * 128].T[:GATE_ROWS]

    z = _dot(h_scr[...], wq_ref[...])
    gain = qg_ref[...] * QK_SCALE
    for r in range(GROUP):
        cols = slice(r * HEAD_DIM, (r + 1) * HEAD_DIM)
        q_ref[r] = _head_rms(z[:, cols], gain).T.astype(BF16)


def _q_proj(x, gain, scale, shift, w_q, w_gate, q_gain, tm=1024):
    T, D = x.shape
    tn = GROUP * HEAD_DIM
    vec = pl.BlockSpec((1, D), lambda i, j: (0, 0))
    ng = w_gate.shape[1]
    return pl.pallas_call(
        _q_kernel,
        grid=(T // tm, N_KV),
        in_specs=[pl.BlockSpec((tm, D), lambda i, j: (i, 0)), vec, vec, vec,
                  pl.BlockSpec((D, tn), lambda i, j: (0, j)),
                  pl.BlockSpec((D, ng), lambda i, j: (0, 0)),
                  pl.BlockSpec((1, HEAD_DIM), lambda i, j: (0, 0))],
        out_specs=[pl.BlockSpec((GROUP, HEAD_DIM, tm), lambda i, j: (j, 0, i)),
                   pl.BlockSpec((N_KV, GATE_ROWS, tm), lambda i, j: (0, 0, i))],
        out_shape=[jax.ShapeDtypeStruct((N_HEADS, HEAD_DIM, T), BF16),
                   jax.ShapeDtypeStruct((N_KV, GATE_ROWS, T), F32)],
        scratch_shapes=[pltpu.VMEM((tm, D), BF16)],
        compiler_params=_params(("arbitrary", "arbitrary"), 48),
        name="q_proj",
    )(x, gain, scale, shift, w_q, w_gate, q_gain.reshape(1, HEAD_DIM))


def _split3_bf16(x):
    hi = x.astype(BF16)
    r1 = x - hi.astype(F32)
    mid = r1.astype(BF16)
    lo = (r1 - mid.astype(F32)).astype(BF16)
    return hi, mid, lo


def _softmax_over_keys(s, valid):
    s = jnp.where(valid, s, NEG_INF)
    m = jnp.max(s, axis=0, keepdims=True)
    e = jnp.where(valid, jnp.exp(s - m), 0.0)
    return e / jnp.maximum(jnp.sum(e, axis=0, keepdims=True), 1e-30)


def _attn_t_kernel(q_ref, gt_ref, sl_ref, qx_ref, kc_ref, vc_ref, ks_ref, vs_ref, kw_ref, vw_ref,
                   o_ref, imp_scr, keep_scr, nk_scr, qa_scr, m_scr, l_scr, acc_scr, kept_smem, *, n_cmp, n_slc):
    qb = pl.program_id(1)
    t0 = qb * Q_BLOCK
    lanes = GROUP * Q_BLOCK
    slope = sl_ref[0]
    q_t = jnp.concatenate([q_ref[r] for r in range(GROUP)], axis=1)
    t_rel = lax.broadcasted_iota(jnp.int32, (1, lanes), 1) % Q_BLOCK

    s = _dot(kc_ref[0], q_t)
    ci = lax.broadcasted_iota(jnp.int32, (n_cmp, lanes), 0)
    d = ((t0 + t_rel) - (ci * CMP_STRIDE + (CMP_LEN - 1))).astype(F32)
    p = _softmax_over_keys(s - slope * d, d >= 0)
    o_cmp = _dot(vc_ref[0], p.astype(BF16))

    p_group = p[:, 0:Q_BLOCK]
    for r in range(1, GROUP):
        p_group = p_group + p[:, r * Q_BLOCK:(r + 1) * Q_BLOCK]
    oj = lax.broadcasted_iota(jnp.int32, (n_slc, n_cmp), 0) * (SLC_LEN // CMP_STRIDE)
    oi = lax.broadcasted_iota(jnp.int32, (n_slc, n_cmp), 1)
    overlap = ((oi > oj - CMP_LEN // CMP_STRIDE) & (oi < oj + SLC_LEN // CMP_STRIDE)).astype(BF16)
    imp = sum(_dot(overlap, part) for part in _split3_bf16(p_group))
    blk = lax.broadcasted_iota(jnp.int32, (n_slc, Q_BLOCK), 0)
    cur = (t0 + lax.broadcasted_iota(jnp.int32, (n_slc, Q_BLOCK), 1)) // SLC_LEN
    forced = (blk == 0) | ((blk <= cur) & (blk > cur - N_FORCED_LOCAL))
    imp = jnp.where(forced, imp + FORCE_BONUS, imp)
    imp = jnp.where(blk <= cur, imp, NEG_INF)

    imp_scr[...] = imp
    keep_scr[...] = jnp.zeros(keep_scr.shape, F32)
    sub = lax.broadcasted_iota(jnp.int32, (8, Q_BLOCK), 0)
    last_blk = (t0 + Q_BLOCK - 1) // SLC_LEN
    rank_step = max(n_slc // RANK_VARIANTS, 8)

    def count_leading(n_lead):
        n_grp = n_lead // 8
        imp_rows = [imp_scr[8 * v:8 * v + 8, :] for v in range(n_grp)]
        rank = [jnp.zeros((8, Q_BLOCK), jnp.int32) for _ in range(n_grp)]
        for jp in range(n_lead):
            other = imp_scr[jp:jp + 1, :]
            for v in range(n_grp):
                if 8 * v > jp:
                    beats = other >= imp_rows[v]
                elif 8 * v + 7 < jp:
                    beats = other > imp_rows[v]
                else:
                    beats = (other > imp_rows[v]) | ((other == imp_rows[v]) & (sub > jp - 8 * v))
                rank[v] = rank[v] + jnp.where(beats, 1, 0)
        rank = jnp.concatenate(rank, axis=0)
        lead_blk = lax.broadcasted_iota(jnp.int32, (n_lead, Q_BLOCK), 0)
        lead_cur = (t0 + lax.broadcasted_iota(jnp.int32, (n_lead, Q_BLOCK), 1)) // SLC_LEN
        keep = (rank < SLC_TOPK) & (lead_blk <= lead_cur)
        keep_scr[0:n_lead, :] = jnp.where(keep, 1.0, 0.0)

    for n_lead in range(rank_step, n_slc + 1, rank_step):
        @pl.when(last_blk // rank_step == n_lead // rank_step - 1)
        def _(n_lead=n_lead):
            count_leading(n_lead)

    not_kept = jnp.where(keep_scr[...] > 0.0, 0.0, NEG_INF).astype(BF16)
    for r in range(GROUP):
        nk_scr[:, r * Q_BLOCK:(r + 1) * Q_BLOCK] = not_kept

    qa_scr[0:HEAD_DIM, :] = q_t
    qa_scr[HEAD_DIM + AUG_BLOCKS:, :] = qx_ref[0]

    m_scr[...] = jnp.full(m_scr.shape, NEG_INF, F32)
    l_scr[...] = jnp.zeros(l_scr.shape, F32)
    acc_scr[...] = jnp.zeros(acc_scr.shape, F32)
    krow = lax.broadcasted_iota(jnp.int32, (KV_TILE, lanes), 0)
    n_full = t0 // KV_TILE

    def scores(kt):
        k0 = pl.multiple_of(kt * KV_TILE, KV_TILE)
        first_blk = pl.multiple_of((kt * (KV_TILE // SLC_LEN) // AUG_BLOCKS) * AUG_BLOCKS, AUG_BLOCKS)
        qa_scr[HEAD_DIM:HEAD_DIM + AUG_BLOCKS, :] = nk_scr[pl.ds(first_blk, AUG_BLOCKS), :]
        return _dot(ks_ref[0, pl.ds(k0, KV_TILE), :], qa_scr[...])

    def accumulate(s, kt, causal):
        k0 = pl.multiple_of(kt * KV_TILE, KV_TILE)
        off = t_rel + (t0 - k0)
        if causal:
            s = jnp.where(krow <= off, s, NEG_INF)
        col_term = slope * off.astype(F32)
        m_old = m_scr[...]
        m_new = jnp.maximum(m_old, jnp.max(s, axis=0, keepdims=True) - col_term)
        e = jnp.exp(s - (m_new + col_term))
        alpha = jnp.exp(m_old - m_new)
        l_scr[...] = alpha * l_scr[...] + jnp.sum(e, axis=0, keepdims=True)
        acc_scr[...] = alpha * acc_scr[...] + _dot(vs_ref[0, :, pl.ds(k0, KV_TILE)], e.astype(BF16))
        m_scr[...] = m_new

    blocks_per_tile = KV_TILE // SLC_LEN
    for tile in range(n_slc // blocks_per_tile):
        kept_smem[tile] = jnp.sum(keep_scr[tile * blocks_per_tile:(tile + 1) * blocks_per_tile, :])

    def full_tile(kt, carry):
        @pl.when(kept_smem[kt] > 0.0)
        def _():
            accumulate(scores(kt), kt, causal=False)

        return carry

    lax.fori_loop(0, n_full, full_tile, 0)
    accumulate(scores(n_full), n_full, causal=True)
    o_slc = acc_scr[...] / l_scr[...]

    w0 = pl.multiple_of(jnp.maximum(t0 - WINDOW, 0), Q_BLOCK)
    s = _dot(kw_ref[0, pl.ds(w0, WIN_KEYS), :], q_t)
    wrow = lax.broadcasted_iota(jnp.int32, (WIN_KEYS, lanes), 0)
    d = ((t0 - w0) + t_rel - wrow).astype(F32)
    p = _softmax_over_keys(s - slope * d, (d >= 0) & (d < WINDOW))
    o_win = _dot(vw_ref[0, :, pl.ds(w0, WIN_KEYS)], p.astype(BF16))

    gates = gt_ref[0]

    def gate_lanes(branch):
        return jnp.concatenate([gates[branch * GROUP + r:branch * GROUP + r + 1, :] for r in range(GROUP)], axis=1)

    o_t = gate_lanes(0) * o_cmp + gate_lanes(1) * o_slc + gate_lanes(2) * o_win
    for r in range(GROUP):
        o_ref[:, r * HEAD_DIM:(r + 1) * HEAD_DIM] = o_t[:, r * Q_BLOCK:(r + 1) * Q_BLOCK].T.astype(BF16)


def _attention_t(q_t, gates_t, slopes, q_extra, k_cmp, v_cmp_t, k_slc_aug, kv, v_t):
    T = q_t.shape[2]
    n_cmp = T // CMP_STRIDE
    n_slc = T // SLC_LEN
    lanes = GROUP * Q_BLOCK
    hw = GROUP * HEAD_DIM
    n_extra = q_extra.shape[1]

    def per_group(shape):
        return pl.BlockSpec((1,) + shape, lambda g, qb: (g, 0, 0))

    return pl.pallas_call(
        functools.partial(_attn_t_kernel, n_cmp=n_cmp, n_slc=n_slc),
        grid=(N_KV, T // Q_BLOCK),
        in_specs=[pl.BlockSpec((GROUP, HEAD_DIM, Q_BLOCK), lambda g, qb: (g, 0, qb)),
                  pl.BlockSpec((1, GATE_ROWS, Q_BLOCK), lambda g, qb: (g, 0, qb)),
                  per_group((1, lanes)),
                  per_group((n_extra, lanes)),
                  per_group((n_cmp, HEAD_DIM)),
                  per_group((HEAD_DIM, n_cmp)),
                  per_group((T, 2 * HEAD_DIM)),
                  per_group((HEAD_DIM, T)),
                  pl.BlockSpec((1, T, HEAD_DIM), lambda g, qb: (4 * N_KV + g, 0, 0)),
                  pl.BlockSpec((1, HEAD_DIM, T), lambda g, qb: (N_KV + g, 0, 0))],
        out_specs=pl.BlockSpec((Q_BLOCK, hw), lambda g, qb: (qb, g)),
        out_shape=jax.ShapeDtypeStruct((T, N_HEADS * HEAD_DIM), BF16),
        scratch_shapes=[pltpu.VMEM((n_slc, Q_BLOCK), F32),
                        pltpu.VMEM((n_slc, Q_BLOCK), F32),
                        pltpu.VMEM((n_slc, lanes), BF16),
                        pltpu.VMEM((2 * HEAD_DIM, lanes), BF16),
                        pltpu.VMEM((1, lanes), F32),
                        pltpu.VMEM((1, lanes), F32),
                        pltpu.VMEM((HEAD_DIM, lanes), F32),
                        pltpu.SMEM((T // KV_TILE,), F32)],
        compiler_params=_params(("arbitrary", "arbitrary"), 48),
        name="nsa_attention",
    )(q_t, gates_t, slopes, q_extra, k_cmp, v_cmp_t, k_slc_aug, v_t, kv, v_t)


def _gate_weight(w_in):
    D = w_in.shape[0]
    wg = w_in[:, N_HEADS * HEAD_DIM:].reshape(D, N_BRANCH, N_KV, GROUP).transpose(0, 2, 1, 3)
    wg = wg.reshape(D, N_KV, N_BRANCH * GROUP)
    wg = jnp.pad(wg, ((0, 0), (0, 0), (0, 128 - N_BRANCH * GROUP)))
    return wg.reshape(D, N_KV * 128).astype(BF16)


def _alibi_and_mask_columns(T):
    h = jnp.arange(1, N_HEADS + 1, dtype=F32)
    slopes = (2.0 ** (-8.0 * h / N_HEADS)).reshape(N_KV, GROUP)
    slopes = jnp.repeat(slopes, Q_BLOCK, axis=1)[:, None, :]
    hi, mid, lo = _split3_bf16(slopes)
    base = float(AUG_OFFSET_BASE)
    rows = [(hi.astype(F32) * base).astype(BF16), (mid.astype(F32) * base).astype(BF16),
            (lo.astype(F32) * base).astype(BF16), hi, mid, lo]
    n_pad = HEAD_DIM - AUG_BLOCKS - len(rows)
    q_extra = jnp.concatenate(rows + [jnp.zeros((N_KV, n_pad, GROUP * Q_BLOCK), BF16)], axis=1)

    kpos = jnp.arange(T, dtype=jnp.int32)
    one_hot = ((kpos // SLC_LEN) % AUG_BLOCKS)[:, None] == jnp.arange(AUG_BLOCKS, dtype=jnp.int32)[None, :]
    off = kpos % KV_TILE
    a = (off // AUG_OFFSET_BASE)[:, None]
    b = (off % AUG_OFFSET_BASE)[:, None]
    k_extra = jnp.concatenate([one_hot.astype(jnp.int32), a, a, a, b, b, b,
                               jnp.zeros((T, n_pad), jnp.int32)], axis=1).astype(BF16)
    return slopes, q_extra, k_extra


def _conv_ffn_layer(x, m, norm_gain, w_up, conv_w, conv_b, w_down):
    D = D_MODEL
    sh2, sc2, g2 = m[:, 3 * D:4 * D], m[:, 4 * D:5 * D], m[:, 5 * D:6 * D]
    act = _ffn_up(x, norm_gain.reshape(1, D), sc2, sh2, w_up.astype(BF16), conv_w, conv_b)
    return _mm_res(act, w_down.astype(BF16), x, g2)


def kernel(x, c, mod_w, mod_b, norm_gain, ffn_w_up, ffn_conv_w, ffn_conv_b, ffn_w_down, sgu_w_in, sgu_v_gain, sgu_w_s, sgu_b_s, sgu_w_out, kv_gain, kv_mod_w, kv_mod_b, w_kv, cmp_pos, cmp_w1, cmp_w2, k_gain, nsa_w_in, nsa_q_gain, nsa_w_o):
    B, T, D = x.shape
    assert B == 1 and D == D_MODEL and T % 1024 == 0
    xs = x[0]
    c_col = c.reshape(D, 1)
    mods = _adaln(c_col, mod_w, mod_b)
    kv_mod = _adaln(c_col, kv_mod_w[None], kv_mod_b[None])[0]

    slopes, q_extra, k_extra = _alibi_and_mask_columns(T)
    kv = v_t = k_cmp = v_cmp_t = k_slc_aug = None
    for layer in range(DEPTH):
        m = mods[layer]
        sh1, sc1, g1 = m[:, 0:D], m[:, D:2 * D], m[:, 2 * D:3 * D]
        gain1 = norm_gain[layer, 0].reshape(1, D)
        if layer == N_A_LAYERS:
            key_gain = jnp.ones((N_KV_TENSORS, 1, HEAD_DIM), F32).at[2, 0].set(k_gain[1]).at[4, 0].set(k_gain[2])
            kv, v_t = _kv_proj(xs, kv_gain.reshape(1, D), kv_mod[:, D:], kv_mod[:, :D], w_kv.astype(BF16), key_gain)
            v_t = v_t.reshape(2 * N_KV, HEAD_DIM, T)
            k_cmp = _compress(kv, 0, cmp_pos, cmp_w1, cmp_w2, k_gain[0])
            v_cmp_t = _compress(kv, 1, cmp_pos, cmp_w1, cmp_w2, k_gain[0])
            k_slc_aug = jnp.concatenate(
                [kv[2 * N_KV:3 * N_KV], jnp.broadcast_to(k_extra, (N_KV, T, HEAD_DIM))], axis=-1)
        if layer < N_A_LAYERS:
            a = layer
            uv, ss = _sgu_in(xs, gain1, sc1, sh1, sgu_w_in[a].astype(BF16))
            y = _sgu_gate(uv, ss, sgu_v_gain[a], sgu_w_s[a], sgu_b_s[a])
            xs = _mm_res(y, sgu_w_out[a].astype(BF16), xs, g1)
        else:
            bl = layer - N_A_LAYERS
            w_in = nsa_w_in[bl]
            q_t, gates_t = _q_proj(xs, gain1, sc1, sh1, w_in[:, :N_HEADS * HEAD_DIM].astype(BF16),
                                   _gate_weight(w_in), nsa_q_gain[bl])
            o = _attention_t(q_t, gates_t, slopes, q_extra, k_cmp, v_cmp_t, k_slc_aug, kv, v_t)
            xs = _mm_res(o, nsa_w_o[bl].astype(BF16), xs, g1)
        xs = _conv_ffn_layer(xs, m, norm_gain[layer, 1], ffn_w_up[layer], ffn_conv_w[layer],
                             ffn_conv_b[layer], ffn_w_down[layer])
    return xs[None]
```
